```python
import math
import jax, jax.numpy as jnp
from jax import lax
import numpy as np

D_MODEL = 1024
BATCH = 8
SEQ = 4096
DEPTH = 4

H_A = 4
DK_A = 64
DV_A = 128
CONV_A = 4
MLSTM_CHUNK = 64
F_BIAS_INIT = 3.0
H_B = 8
DH_B = 64
R_KV = 128
H_IDX = 4
D_IDX = 64
TOPK_MAX = 256
TOPK_DIV = 4
Q_BLOCK = 128
H_C = 4
DK_C = 64
DV_C = 128
GLA_RANK = 16
GLA_TAU = 16.0
GLA_CHUNK = 64
H_D = 8
DH_D = 64
DIL_PATTERNS = ((128, 1), (512, 4), (2048, 16))
N_BUCKETS = 32
MAX_DIST = 2048
N_REL_HEADS = 8
N_GROUPS = 4
E_PER_GROUP = 4
TOP_K_FINE = 2
EXPERT_FF = 512

EPS = 1e-6
NEG_INF = -1e30
N_AB = (DEPTH + 1) // 2
N_CD = DEPTH // 2
AB_SIZES = (H_A * DK_A, H_A * DK_A, H_A * DV_A, H_A * DV_A, H_A, H_A, H_B * DH_B, R_KV, H_IDX * D_IDX, D_IDX, H_IDX)
CD_SIZES = (H_C * DK_C, H_C * DK_C, H_C * DV_C, GLA_RANK, H_C * DV_C, H_D * DH_D, H_D * DH_D, H_D * DH_D)
W_AB = sum(AB_SIZES)
W_CD = sum(CD_SIZES)
MIX_AB = H_A * DV_A + H_B * DH_B
MIX_CD = H_C * DV_C + H_D * DH_D

kernel_name = "hybrid_mlstm_dsa_gla_dilated_hmoe"


def rmsnorm(x, g):
    xf = x.astype(jnp.float32)
    y = xf * lax.rsqrt(jnp.mean(xf * xf, axis=-1, keepdims=True) + EPS)
    return (y * g.astype(jnp.float32)).astype(x.dtype)


def split_cols(p, sizes):
    offs = np.cumsum((0,) + tuple(sizes))
    return [p[..., int(a):int(b)] for a, b in zip(offs[:-1], offs[1:])]


def heads(a, n):
    return a.reshape(a.shape[0], a.shape[1], n, -1)


def rel_bucket(dist):
    n = jnp.maximum(dist, 0)
    max_exact = N_BUCKETS // 2
    nf = jnp.maximum(n, 1).astype(jnp.float32)
    large = max_exact + (jnp.log(nf / max_exact) / math.log(MAX_DIST / max_exact)
                         * (N_BUCKETS - max_exact)).astype(jnp.int32)
    large = jnp.minimum(large, N_BUCKETS - 1)
    return jnp.where(n < max_exact, n, large)


def causal_dwconv(x, w, b):
    K = w.shape[0]
    L = x.shape[1]
    xp = jnp.pad(x, ((0, 0), (K - 1, 0), (0, 0)))
    return sum(xp[:, j:j + L] * w[j] for j in range(K)) + b


def mlstm_chunkwise(q, k, v, i_pre, f_pre):
    Bn, L, H, DK = q.shape
    DV = v.shape[-1]
    T = MLSTM_CHUNK
    NC = L // T

    def chunks(a):
        return jnp.moveaxis(a.reshape((Bn, NC, T) + a.shape[2:]), 1, 0)

    causal = jnp.tril(jnp.ones((T, T), bool))
    log_f = jax.nn.log_sigmoid(f_pre)

    def step(carry, xs):
        Cm, n, m = carry
        qc, kc, vc, li, lf = xs
        b = jnp.cumsum(lf, axis=1)
        Dm = jnp.where(causal[None, :, :, None],
                       b[:, :, None, :] - b[:, None, :, :] + li[:, None, :, :], NEG_INF)
        g = b + m[:, None, :]
        mt = jnp.maximum(g, Dm.max(axis=2))
        w_intra = jnp.exp(Dm - mt[:, :, None, :])
        w_state = jnp.exp(g - mt)
        s = jnp.einsum('bthk,bshk->btsh', qc, kc) * w_intra
        num = (w_state[..., None] * jnp.einsum('bthk,bhkv->bthv', qc, Cm)
               + jnp.einsum('btsh,bshv->bthv', s, vc))
        den = w_state * jnp.einsum('bthk,bhk->bth', qc, n) + s.sum(axis=2)
        h = num / jnp.maximum(jnp.abs(den), jnp.exp(-mt))[..., None]
        bL = b[:, -1]
        d_end = bL[:, None, :] - b + li
        m_new = jnp.maximum(bL + m, d_end.max(axis=1))
        w_e = jnp.exp(d_end - m_new[:, None, :])
        decay = jnp.exp(bL + m - m_new)
        Cm = decay[..., None, None] * Cm + jnp.einsum('bsh,bshk,bshv->bhkv', w_e, kc, vc)
        n = decay[..., None] * n + jnp.einsum('bsh,bshk->bhk', w_e, kc)
        return (Cm, n, m_new), h

    init = (jnp.zeros((Bn, H, DK, DV), jnp.float32), jnp.zeros((Bn, H, DK), jnp.float32),
            jnp.full((Bn, H), NEG_INF, jnp.float32))
    _, hs = lax.scan(step, init, (chunks(q), chunks(k), chunks(v), chunks(i_pre), chunks(log_f)))
    return jnp.moveaxis(hs, 0, 1).reshape(Bn, L, H, DV)


def gla_chunkwise(q, k, v, log_a):
    Bn, L, H, DK = q.shape
    DV = v.shape[-1]
    T = GLA_CHUNK
    NC = L // T

    def chunks(a):
        return jnp.moveaxis(a.reshape((Bn, NC, T) + a.shape[2:]), 1, 0)

    causal = jnp.tril(jnp.ones((T, T), bool))

    def step(S, xs):
        qc, kc, vc, la = xs
        b = jnp.cumsum(la, axis=1)
        inter = jnp.einsum('bthk,bhkv->bthv', qc * jnp.exp(b), S)
        decay = jnp.exp(jnp.where(causal[None, :, :, None, None], b[:, :, None] - b[:, None, :], NEG_INF))
        A = jnp.einsum('bthk,btshk,bshk->btsh', qc, decay, kc)
        o = inter + jnp.einsum('btsh,bshv->bthv', A, vc)
        bL = b[:, -1]
        S = jnp.exp(bL)[..., None] * S + jnp.einsum('bshk,bshv->bhkv', kc * jnp.exp(bL[:, None] - b), vc)
        return S, o

    _, os_ = lax.scan(step, jnp.zeros((Bn, H, DK, DV), jnp.float32),
                      (chunks(q), chunks(k), chunks(v), chunks(log_a)))
    return jnp.moveaxis(os_, 0, 1).reshape(Bn, L, H, DV)


def dsa_attention(q_abs, c_kv, q_idx, k_idx, w_idx, w_uv, rel_bias):
    Bn, L, H, R = q_abs.shape
    topk = min(TOPK_MAX, L // TOPK_DIV)
    NB = L // Q_BLOCK
    key_pos = jnp.arange(L)
    gather = jax.vmap(lambda tab, idx: tab[idx])

    def block(blk):
        t0 = blk * Q_BLOCK
        qa = lax.dynamic_slice_in_dim(q_abs, t0, Q_BLOCK, axis=1)
        qi = lax.dynamic_slice_in_dim(q_idx, t0, Q_BLOCK, axis=1)
        wi = lax.dynamic_slice_in_dim(w_idx, t0, Q_BLOCK, axis=1)
        pos = t0 + jnp.arange(Q_BLOCK)
        rel = jax.nn.relu(jnp.einsum('bthd,bsd->bths', qi, k_idx))
        score = jnp.einsum('bths,bth->bts', rel, wi).astype(jnp.float32)
        score = jnp.where(key_pos[None, None, :] <= pos[None, :, None], score, -jnp.inf)
        _, idx = lax.top_k(score, topk)
        c_sel = gather(c_kv, idx)
        dist = pos[None, :, None] - idx
        bias = rel_bias[rel_bucket(dist)].transpose(0, 1, 3, 2)
        logits = jnp.einsum('bthr,btkr->bthk', qa, c_sel).astype(jnp.float32) + bias.astype(jnp.float32)
        logits = jnp.where((dist >= 0)[:, :, None, :], logits, NEG_INF)
        p = jax.nn.softmax(logits, axis=-1).astype(c_kv.dtype)
        o_lat = jnp.einsum('bthk,btkr->bthr', p, c_sel)
        return jnp.einsum('bthr,rhd->bthd', o_lat, w_uv)

    out = lax.map(block, jnp.arange(NB))
    return jnp.moveaxis(out, 0, 1).reshape(Bn, L, H, -1)


def dilated_partial(q, k, v, rel_bias, window, dil):
    Bn, L, H, Dh = q.shape
    W = window // dil
    M = L // dil
    nb = -(-M // W)
    Mp = nb * W

    def to_sub(a):
        a = a.reshape(Bn, M, dil, H, Dh).transpose(0, 2, 1, 3, 4)
        a = jnp.pad(a, ((0, 0), (0, 0), (0, Mp - M), (0, 0), (0, 0)))
        return a.reshape(Bn, dil, nb, W, H, Dh)

    def with_prev(a):
        prev = jnp.pad(a[:, :, :-1], ((0, 0), (0, 0), (1, 0), (0, 0), (0, 0), (0, 0)))
        return jnp.concatenate([prev, a], axis=3)

    qs = to_sub(q)
    kc = with_prev(to_sub(k))
    vc = with_prev(to_sub(v))
    tap = jnp.arange(W)[:, None] + W - jnp.arange(2 * W)[None, :]
    tap_ok = (tap >= 0) & (tap <= W)
    blk_ok = (jnp.arange(nb)[:, None] > 0) | (jnp.arange(2 * W)[None, :] >= W)
    mask = tap_ok[None] & blk_ok[:, None, :]
    bias = rel_bias[rel_bucket(jnp.clip(tap, 0, W) * dil)].transpose(2, 0, 1).astype(jnp.float32)
    logits = jnp.einsum('brnqhd,brnkhd->brnhqk', qs, kc).astype(jnp.float32) + bias
    logits = jnp.where(mask[:, None], logits, NEG_INF)
    m = logits.max(axis=-1)
    p = jnp.exp(logits - m[..., None])
    den = p.sum(axis=-1)
    num = jnp.einsum('brnhqk,brnkhd->brnqhd', p, vc.astype(jnp.float32))

    def from_sub(a):
        a = a.reshape((Bn, dil, Mp) + a.shape[4:])[:, :, :M]
        a = jnp.swapaxes(a, 1, 2)
        return a.reshape((Bn, L) + a.shape[3:])

    return from_sub(num), from_sub(jnp.swapaxes(m, 3, 4)), from_sub(jnp.swapaxes(den, 3, 4))


def dilated_mixture(q, k, v, rel_bias):
    parts = [dilated_partial(q, k, v, rel_bias, w, d) for (w, d) in DIL_PATTERNS]
    m_all = jnp.max(jnp.stack([pt[1] for pt in parts]), axis=0)
    num = sum(pt[0] * jnp.exp(pt[1] - m_all)[..., None] for pt in parts)
    den = sum(pt[2] * jnp.exp(pt[1] - m_all) for pt in parts)
    return num / den[..., None]


def mixer_ab(h, w_in, conv_w, conv_b, gate_b, hnorm_g, w_uk, w_uv, rel_bias, w_out):
    Bn, L, _ = h.shape
    p = h @ w_in
    qa, ka, va, oa, ia, fa, qb, ckv, qi, ki, wi = split_cols(p, AB_SIZES)
    qk = jax.nn.silu(causal_dwconv(jnp.concatenate([qa, ka], axis=-1), conv_w, conv_b))
    qa, ka = qk[..., :H_A * DK_A], qk[..., H_A * DK_A:]
    ha = mlstm_chunkwise(heads(qa, H_A).astype(jnp.float32) * DK_A ** -0.5,
                         heads(ka, H_A).astype(jnp.float32),
                         heads(va, H_A).astype(jnp.float32),
                         (ia + gate_b[0]).astype(jnp.float32),
                         (fa + gate_b[1]).astype(jnp.float32))
    ha = rmsnorm(ha.astype(h.dtype), hnorm_g.reshape(H_A, DV_A)).reshape(Bn, L, -1)
    out_a = ha * jax.nn.sigmoid(oa)
    q_abs = jnp.einsum('blhd,rhd->blhr', heads(qb, H_B), w_uk) * DH_B ** -0.5
    out_b = dsa_attention(q_abs, ckv, heads(qi, H_IDX), ki, wi * H_IDX ** -0.5, w_uv, rel_bias)
    y = jnp.concatenate([out_a, out_b.reshape(Bn, L, -1).astype(h.dtype)], axis=-1)
    return y @ w_out


def mixer_cd(h, w_in, w_alpha, b_alpha, hnorm_g, rel_bias, w_out):
    Bn, L, _ = h.shape
    p = h @ w_in
    qc, kc, vc, gc, rc, qd, kd, vd = split_cols(p, CD_SIZES)
    log_a = jax.nn.log_sigmoid((gc @ w_alpha + b_alpha).astype(jnp.float32)) / GLA_TAU
    oc = gla_chunkwise(heads(qc, H_C).astype(jnp.float32) * DK_C ** -0.5,
                       heads(kc, H_C).astype(jnp.float32),
                       heads(vc, H_C).astype(jnp.float32),
                       heads(log_a, H_C))
    oc = rmsnorm(oc.astype(h.dtype), hnorm_g.reshape(H_C, DV_C)).reshape(Bn, L, -1) * jax.nn.silu(rc)
    od = dilated_mixture(heads(qd, H_D) * DH_D ** -0.5, heads(kd, H_D), heads(vd, H_D), rel_bias)
    y = jnp.concatenate([oc, od.reshape(Bn, L, -1).astype(h.dtype)], axis=-1)
    return y @ w_out


def hier_moe(h, w_coarse, b_coarse, w_fine, b_fine, w_gate, w_up, w_down):
    pc = jax.nn.softmax((h @ w_coarse + b_coarse).astype(jnp.float32), axis=-1)
    g_idx = jnp.argmax(pc, axis=-1)
    p_g = jnp.max(pc, axis=-1)
    fine = (jnp.einsum('bld,gde->blge', h, w_fine) + b_fine).astype(jnp.float32)
    fine_sel = jnp.take_along_axis(fine, g_idx[..., None, None], axis=2)[..., 0, :]
    top_v, top_i = lax.top_k(fine_sel, TOP_K_FINE)
    p_e = jax.nn.softmax(top_v, axis=-1)
    w_e = jnp.sum(jax.nn.one_hot(top_i, E_PER_GROUP, dtype=jnp.float32) * p_e[..., None], axis=-2)
    gates = (p_g[..., None, None] * jax.nn.one_hot(g_idx, N_GROUPS, dtype=jnp.float32)[..., None]
             * w_e[..., None, :]).astype(h.dtype)
    y = jnp.zeros_like(h)
    for g in range(N_GROUPS):
        a = jax.nn.silu(jnp.einsum('bld,edf->blef', h, w_gate[g])) * jnp.einsum('bld,edf->blef', h, w_up[g])
        y = y + jnp.einsum('blef,efd->bld', a * gates[:, :, g, :, None], w_down[g])
    return y


def setup_inputs(seed: int = 0) -> dict:
    key = jax.random.key(seed)
    ks = jax.random.split(key, 32)
    f32 = jnp.float32

    def nrm(k, shape, scale):
        return jax.random.normal(k, shape, f32) * scale

    D = D_MODEL
    gate_base = jnp.array([0.0, F_BIAS_INIT], f32)[None, :, None]
    return {
        "x": nrm(ks[0], (BATCH, SEQ, D), 1.0),
        "c": nrm(ks[1], (BATCH, D), 1.0),
        "w_ada": nrm(ks[2], (DEPTH, D, 6 * D), 0.5 * D ** -0.5),
        "b_ada": nrm(ks[3], (DEPTH, 6 * D), 0.02),
        "g_mix": 1.0 + nrm(ks[4], (DEPTH, D), 0.05),
        "g_ffn": 1.0 + nrm(ks[5], (DEPTH, D), 0.05),
        "g_final": 1.0 + nrm(ks[6], (D,), 0.05),
        "rel_bias": nrm(ks[7], (N_BUCKETS, N_REL_HEADS), 0.5),
        "ab_w_in": nrm(ks[8], (N_AB, D, W_AB), D ** -0.5),
        "ab_conv_w": nrm(ks[9], (N_AB, CONV_A, 2 * H_A * DK_A), CONV_A ** -0.5),
        "ab_conv_b": nrm(ks[10], (N_AB, 2 * H_A * DK_A), 0.02),
        "ab_gate_b": gate_base + nrm(ks[11], (N_AB, 2, H_A), 0.1),
        "ab_hnorm_g": 1.0 + nrm(ks[12], (N_AB, H_A * DV_A), 0.05),
        "ab_w_uk": nrm(ks[13], (N_AB, R_KV, H_B, DH_B), R_KV ** -0.5),
        "ab_w_uv": nrm(ks[14], (N_AB, R_KV, H_B, DH_B), R_KV ** -0.5),
        "ab_w_out": nrm(ks[15], (N_AB, MIX_AB, D), MIX_AB ** -0.5),
        "cd_w_in": nrm(ks[16], (N_CD, D, W_CD), D ** -0.5),
        "cd_w_alpha": nrm(ks[17], (N_CD, GLA_RANK, H_C * DK_C), GLA_RANK ** -0.5),
        "cd_b_alpha": nrm(ks[18], (N_CD, H_C * DK_C), 0.1),
        "cd_hnorm_g": 1.0 + nrm(ks[19], (N_CD, H_C * DV_C), 0.05),
        "cd_w_out": nrm(ks[20], (N_CD, MIX_CD, D), MIX_CD ** -0.5),
        "moe_w_coarse": nrm(ks[21], (DEPTH, D, N_GROUPS), D ** -0.5),
        "moe_b_coarse": nrm(ks[22], (DEPTH, N_GROUPS), 0.01),
        "moe_w_fine": nrm(ks[23], (DEPTH, N_GROUPS, D, E_PER_GROUP), D ** -0.5),
        "moe_b_fine": nrm(ks[24], (DEPTH, N_GROUPS, E_PER_GROUP), 0.01),
        "moe_w_gate": nrm(ks[25], (DEPTH, N_GROUPS, E_PER_GROUP, D, EXPERT_FF), D ** -0.5),
        "moe_w_up": nrm(ks[26], (DEPTH, N_GROUPS, E_PER_GROUP, D, EXPERT_FF), D ** -0.5),
        "moe_w_down": nrm(ks[27], (DEPTH, N_GROUPS, E_PER_GROUP, EXPERT_FF, D), EXPERT_FF ** -0.5),
    }


def reference(x, c, w_ada, b_ada, g_mix, g_ffn, g_final, rel_bias,
              ab_w_in, ab_conv_w, ab_conv_b, ab_gate_b, ab_hnorm_g, ab_w_uk, ab_w_uv, ab_w_out,
              cd_w_in, cd_w_alpha, cd_b_alpha, cd_hnorm_g, cd_w_out,
              moe_w_coarse, moe_b_coarse, moe_w_fine, moe_b_fine, moe_w_gate, moe_w_up, moe_w_down):
    mod = jnp.einsum('bd,lde->lbe', jax.nn.silu(c), w_ada) + b_ada[:, None, :]
    for l in range(DEPTH):
        sh_m, sc_m, gt_m, sh_f, sc_f, gt_f = jnp.split(mod[l][:, None, :], 6, axis=-1)
        hm = rmsnorm(x, g_mix[l]) * (1.0 + sc_m) + sh_m
        j = l // 2
        if l % 2 == 0:
            y = mixer_ab(hm, ab_w_in[j], ab_conv_w[j], ab_conv_b[j], ab_gate_b[j], ab_hnorm_g[j],
                         ab_w_uk[j], ab_w_uv[j], rel_bias, ab_w_out[j])
        else:
            y = mixer_cd(hm, cd_w_in[j], cd_w_alpha[j], cd_b_alpha[j], cd_hnorm_g[j], rel_bias, cd_w_out[j])
        x = x + gt_m * y
        hf = rmsnorm(x, g_ffn[l]) * (1.0 + sc_f) + sh_f
        x = x + gt_f * hier_moe(hf, moe_w_coarse[l], moe_b_coarse[l], moe_w_fine[l], moe_b_fine[l],
                                moe_w_gate[l], moe_w_up[l], moe_w_down[l])
    return rmsnorm(x, g_final)
```

```python
import functools
import math

import numpy as np
import jax
import jax.numpy as jnp
from jax import lax
from jax.experimental import pallas as pl
from jax.experimental.pallas import tpu as pltpu

F32 = jnp.float32
BF16 = jnp.bfloat16

EPS = 1e-6
NEG_INF = -1e30

H_A, DK_A, DV_A, CONV_A, MLSTM_CHUNK = 4, 64, 128, 4, 64
H_B, DH_B, R_KV, H_IDX, D_IDX = 8, 64, 128, 4, 64
TOPK_MAX, TOPK_DIV = 256, 4
H_C, DK_C, DV_C, GLA_RANK, GLA_TAU, GLA_CHUNK = 4, 64, 128, 16, 16.0, 64
H_D, DH_D = 8, 64
DIL_PATTERNS = ((128, 1), (512, 4), (2048, 16))
N_BUCKETS, MAX_DIST = 32, 2048
N_GROUPS, E_PER_GROUP, EXPERT_FF = 4, 4, 512

LANES = 128
VMEM_LIMIT = 56 * 1024 * 1024


def _cparams(*sem):
    return pltpu.CompilerParams(dimension_semantics=sem, vmem_limit_bytes=VMEM_LIMIT)


def _bdot(a, b):
    return jnp.dot(a.astype(BF16), b.astype(BF16), preferred_element_type=F32)


def _bdot_nt(a, b):
    return lax.dot_general(a.astype(BF16), b.astype(BF16), (((1,), (1,)), ((), ())),
                           preferred_element_type=F32)


def _fdot(a, b):
    return jnp.dot(a, b, preferred_element_type=F32, precision=lax.Precision.HIGHEST)


def _ada_kernel(c_ref, w_ref, b_ref, o_ref):
    c = c_ref[...]
    a = c * jax.nn.sigmoid(c)
    o_ref[0] = _fdot(a, w_ref[0]) + b_ref[0]


def ada_mod(c, w_ada, b_ada):
    depth, d, e = w_ada.shape
    bn = c.shape[0]
    tn = 1536
    return pl.pallas_call(
        _ada_kernel,
        grid=(depth, e // tn),
        in_specs=[pl.BlockSpec((bn, d), lambda l, j: (0, 0)),
                  pl.BlockSpec((1, d, tn), lambda l, j: (l, 0, j)),
                  pl.BlockSpec((1, 1, tn), lambda l, j: (l, 0, j))],
        out_specs=pl.BlockSpec((1, bn, tn), lambda l, j: (l, 0, j)),
        out_shape=jax.ShapeDtypeStruct((depth, bn, e), F32),
        compiler_params=_cparams("arbitrary", "arbitrary"),
        name="ada_mod",
    )(c, w_ada, b_ada.reshape(depth, 1, e))


def _modulated_norm(x, g, sc, sh):
    y = x * lax.rsqrt(jnp.mean(x * x, axis=-1, keepdims=True) + EPS) * g
    return y * (1.0 + sc) + sh


def _norm_proj_kernel(x_ref, g_ref, sc_ref, sh_ref, w_ref, *o_refs, segs):
    h = _modulated_norm(x_ref[0], g_ref[...], sc_ref[0], sh_ref[0]).astype(BF16)
    for o_ref, (start, width) in zip(o_refs, segs):
        o_ref[0] = jnp.dot(h, w_ref[:, start:start + width], preferred_element_type=F32).astype(o_ref.dtype)


def norm_proj(x, g, sc, sh, w, segs, dtypes, tm=512):
    bn, L, d = x.shape
    wp = w.shape[1]
    kern = functools.partial(_norm_proj_kernel, segs=tuple(segs))
    return pl.pallas_call(
        kern,
        grid=(bn, L // tm),
        in_specs=[pl.BlockSpec((1, tm, d), lambda b, i: (b, i, 0)),
                  pl.BlockSpec((1, d), lambda b, i: (0, 0)),
                  pl.BlockSpec((1, 1, d), lambda b, i: (b, 0, 0)),
                  pl.BlockSpec((1, 1, d), lambda b, i: (b, 0, 0)),
                  pl.BlockSpec((d, wp), lambda b, i: (0, 0))],
        out_specs=[pl.BlockSpec((1, tm, wd), lambda b, i: (b, i, 0)) for (_, wd) in segs],
        out_shape=[jax.ShapeDtypeStruct((bn, L, wd), dt) for (_, wd), dt in zip(segs, dtypes)],
        compiler_params=_cparams("parallel", "parallel"),
        name="norm_proj",
    )(x, g.reshape(1, d), sc.reshape(bn, 1, d), sh.reshape(bn, 1, d), w)


def _outproj_kernel(x_ref, ya_ref, yb_ref, gt_ref, w_ref, o_ref):
    na = ya_ref.shape[-1]
    y = _bdot(ya_ref[0], w_ref[:na, :]) + _bdot(yb_ref[0], w_ref[na:, :])
    o_ref[0] = x_ref[0] + gt_ref[0] * y


def outproj_residual(x, ya, yb, gt, w, tm=512):
    bn, L, d = x.shape
    na, nb = ya.shape[-1], yb.shape[-1]
    return pl.pallas_call(
        _outproj_kernel,
        grid=(bn, L // tm),
        in_specs=[pl.BlockSpec((1, tm, d), lambda b, i: (b, i, 0)),
                  pl.BlockSpec((1, tm, na), lambda b, i: (b, i, 0)),
                  pl.BlockSpec((1, tm, nb), lambda b, i: (b, i, 0)),
                  pl.BlockSpec((1, 1, d), lambda b, i: (b, 0, 0)),
                  pl.BlockSpec((na + nb, d), lambda b, i: (0, 0))],
        out_specs=pl.BlockSpec((1, tm, d), lambda b, i: (b, i, 0)),
        out_shape=jax.ShapeDtypeStruct((bn, L, d), F32),
        compiler_params=_cparams("parallel", "parallel"),
        name="outproj_residual",
    )(x, ya, yb, gt.reshape(bn, 1, d), w)


def _final_norm_kernel(x_ref, g_ref, o_ref):
    x = x_ref[0]
    o_ref[0] = x * lax.rsqrt(jnp.mean(x * x, axis=-1, keepdims=True) + EPS) * g_ref[...]


def final_norm(x, g, tm=512):
    bn, L, d = x.shape
    return pl.pallas_call(
        _final_norm_kernel,
        grid=(bn, L // tm),
        in_specs=[pl.BlockSpec((1, tm, d), lambda b, i: (b, i, 0)),
                  pl.BlockSpec((1, d), lambda b, i: (0, 0))],
        out_specs=pl.BlockSpec((1, tm, d), lambda b, i: (b, i, 0)),
        out_shape=jax.ShapeDtypeStruct((bn, L, d), F32),
        compiler_params=_cparams("parallel", "parallel"),
        name="final_norm",
    )(x, g.reshape(1, d))


def _log_sigmoid(x):
    return jnp.minimum(x, 0.0) - jnp.log1p(jnp.exp(-jnp.abs(x)))


def _dot_tn(a, b):
    return lax.dot_general(a.astype(BF16), b.astype(BF16), (((0,), (0,)), ((), ())),
                           preferred_element_type=F32)


def _tri(T):
    r = np.arange(T)
    return (r[:, None] >= r[None, :]).astype(np.float32)


def _mlstm_kernel(qk_ref, v_ref, og_ref, gt_ref, cw_ref, cb_ref, gb_ref, hg_ref, ltri_ref, utri_ref,
                  o_ref, xbuf, cext, mstate):
    T = qk_ref.shape[1]
    tail = xbuf.shape[0] - T

    @pl.when(pl.program_id(1) == 0)
    def _():
        xbuf[0:tail, :] = jnp.zeros((tail, xbuf.shape[1]), F32)
        cext[...] = jnp.zeros(cext.shape, F32)
        mstate[...] = jnp.full(mstate.shape, NEG_INF, F32)

    x = qk_ref[0]
    xbuf[tail:tail + T, :] = x
    conv = cb_ref[...] + sum(xbuf[pl.ds(tail - (CONV_A - 1) + j, T), :] * cw_ref[j:j + 1, :]
                             for j in range(CONV_A))
    xbuf[0:tail, :] = x[T - tail:T, :]
    qk = conv * jax.nn.sigmoid(conv)

    G = gt_ref[0] + gb_ref[...]
    GT = G.T
    bcol = _fdot(ltri_ref[...], _log_sigmoid(G))
    brow = _fdot(_log_sigmoid(GT), utri_ref[...])
    causal = ltri_ref[...] > 0.5
    lane = lax.broadcasted_iota(jnp.int32, (T, DV_A), 1)
    ones_blk = jnp.where(lane == 0, 1.0, 0.0).astype(F32)
    nq = H_A * DK_A

    for h in range(H_A):
        b_c = bcol[:, H_A + h:H_A + h + 1]
        b_r = brow[H_A + h:H_A + h + 1, :]
        li_r = GT[h:h + 1, :]
        li_c = G[:, h:h + 1]
        m_prev = mstate[h:h + 1, 0:1]
        Dm = jnp.where(causal, b_c - b_r + li_r, NEG_INF)
        g = b_c + m_prev
        mt = jnp.maximum(g, jnp.max(Dm, axis=1, keepdims=True))
        w_intra = jnp.exp(Dm - mt)
        w_state = jnp.exp(g - mt)
        q = qk[:, h * DK_A:(h + 1) * DK_A] * (DK_A ** -0.5)
        k = qk[:, nq + h * DK_A:nq + (h + 1) * DK_A]
        vext = jnp.concatenate([v_ref[0, :, h * DV_A:(h + 1) * DV_A], ones_blk], axis=1)
        s = _bdot_nt(q, k) * w_intra
        nd = w_state * _bdot(q, cext[h]) + _bdot(s, vext)
        num = nd[:, :DV_A]
        den = nd[:, DV_A:DV_A + 1]
        hout = num / jnp.maximum(jnp.abs(den), jnp.exp(-mt))
        bL = b_c[T - 1:T, :]
        d_end = bL - b_c + li_c
        m_new = jnp.maximum(bL + m_prev, jnp.max(d_end, axis=0, keepdims=True))
        w_e = jnp.exp(d_end - m_new)
        decay = jnp.exp(bL + m_prev - m_new)
        cext[h] = decay * cext[h] + _dot_tn(k * w_e, vext)
        mstate[h:h + 1, :] = jnp.broadcast_to(m_new, (1, mstate.shape[1]))
        y = hout * lax.rsqrt(jnp.mean(hout * hout, axis=-1, keepdims=True) + EPS)
        y = y * hg_ref[:, h * DV_A:(h + 1) * DV_A]
        o_ref[0, :, h * DV_A:(h + 1) * DV_A] = (
            y * jax.nn.sigmoid(og_ref[0, :, h * DV_A:(h + 1) * DV_A])).astype(o_ref.dtype)


def mlstm_mixer(qk, v, og, gates, conv_w, conv_b, gate_b, hnorm_g, out_dtype=F32):
    bn, L, _ = qk.shape
    T = MLSTM_CHUNK
    nv = H_A * DV_A
    gb = jnp.zeros((1, LANES), F32).at[0, :2 * H_A].set(gate_b.reshape(-1))
    ltri = jnp.asarray(_tri(T))
    row = lambda b, i: (b, i, 0)
    const = lambda b, i: (0, 0)
    return pl.pallas_call(
        _mlstm_kernel,
        grid=(bn, L // T),
        in_specs=[pl.BlockSpec((1, T, qk.shape[-1]), row),
                  pl.BlockSpec((1, T, nv), row),
                  pl.BlockSpec((1, T, nv), row),
                  pl.BlockSpec((1, T, LANES), row),
                  pl.BlockSpec(conv_w.shape, const),
                  pl.BlockSpec((1, conv_b.shape[-1]), const),
                  pl.BlockSpec((1, LANES), const),
                  pl.BlockSpec((1, nv), const),
                  pl.BlockSpec((T, T), const),
                  pl.BlockSpec((T, T), const)],
        out_specs=pl.BlockSpec((1, T, nv), row),
        out_shape=jax.ShapeDtypeStruct((bn, L, nv), out_dtype),
        scratch_shapes=[pltpu.VMEM((8 + T, qk.shape[-1]), F32),
                        pltpu.VMEM((H_A, DK_A, 2 * DV_A), F32),
                        pltpu.VMEM((8, LANES), F32)],
        compiler_params=_cparams("parallel", "arbitrary"),
        name="mlstm_mixer",
    )(qk, v, og, gates, conv_w, conv_b.reshape(1, -1), gb, hnorm_g.reshape(1, -1), ltri, ltri.T)


def _gla_level_consts(T):
    r = np.arange(T)
    mq, mk, bm = [], [], []
    n = T
    while n >= 2:
        start = (r // n) * n
        mid = start + n // 2
        upper = r >= mid
        mq.append((upper[:, None] & (r[None, :] >= mid[:, None]) & (r[None, :] <= r[:, None])).astype(np.float32))
        mk.append((~upper[:, None] & (r[None, :] > r[:, None]) & (r[None, :] < mid[:, None])).astype(np.float32))
        bm.append((upper[:, None] & ~upper[None, :] & (start[:, None] == start[None, :])).astype(np.float32))
        n //= 2
    bm.append(np.eye(T, dtype=np.float32))
    return np.concatenate(mq, 0), np.concatenate(mk, 0), np.stack(bm, 0)


def _gla_kernel(qk_ref, v_ref, gc_ref, rc_ref, wa_ref, ba_ref, hg_ref, ltri_ref, mq_ref, mk_ref, bm_ref,
                o_ref, st):
    T = qk_ref.shape[1]
    nlev = bm_ref.shape[0] - 1
    nq = H_C * DK_C

    @pl.when(pl.program_id(1) == 0)
    def _():
        st[...] = jnp.zeros(st.shape, F32)

    qk = qk_ref[0]
    q = qk[:, :nq] * (DK_C ** -0.5)
    k = qk[:, nq:]
    la = _log_sigmoid(_bdot(gc_ref[0], wa_ref[...]) + ba_ref[...]) * (1.0 / GLA_TAU)
    eq = jnp.exp(_fdot(mq_ref[...], la))
    ek = jnp.exp(_fdot(mk_ref[...], la))
    bfull = _fdot(ltri_ref[...], la)
    bL = bfull[T - 1:T, :]
    qe = q * jnp.exp(bfull)
    kd = k * jnp.exp(bL - bfull)
    eL = jnp.exp(bL)

    for h in range(H_C):
        cs = slice(h * DK_C, (h + 1) * DK_C)
        vs = slice(h * DV_C, (h + 1) * DV_C)
        qh, kh, vh = q[:, cs], k[:, cs], v_ref[0, :, vs]
        A = bm_ref[nlev] * _bdot_nt(qh, kh)
        for lv in range(nlev):
            A = A + bm_ref[lv] * _bdot_nt(qh * eq[lv * T:(lv + 1) * T, cs], kh * ek[lv * T:(lv + 1) * T, cs])
        o = _bdot_nt(qe[:, cs], st[h]) + _bdot(A, vh)
        st[h] = eL[:, cs] * st[h] + _dot_tn(vh, kd[:, cs])
        y = o * lax.rsqrt(jnp.mean(o * o, axis=-1, keepdims=True) + EPS) * hg_ref[:, vs]
        r = rc_ref[0, :, vs]
        o_ref[0, :, vs] = (y * (r * jax.nn.sigmoid(r))).astype(o_ref.dtype)


def gla_mixer(qk, v, gc, rc, w_alpha, b_alpha, hnorm_g, out_dtype=F32):
    bn, L, _ = qk.shape
    T = GLA_CHUNK
    nv = H_C * DV_C
    nq = H_C * DK_C
    wa = jnp.pad(w_alpha, ((0, gc.shape[-1] - w_alpha.shape[0]), (0, 0))).astype(BF16)
    mq, mk, bm = _gla_level_consts(T)
    row = lambda b, i: (b, i, 0)
    const = lambda b, i: (0, 0)
    return pl.pallas_call(
        _gla_kernel,
        grid=(bn, L // T),
        in_specs=[pl.BlockSpec((1, T, 2 * nq), row),
                  pl.BlockSpec((1, T, nv), row),
                  pl.BlockSpec((1, T, gc.shape[-1]), row),
                  pl.BlockSpec((1, T, nv), row),
                  pl.BlockSpec(wa.shape, const),
                  pl.BlockSpec((1, nq), const),
                  pl.BlockSpec((1, nv), const),
                  pl.BlockSpec((T, T), const),
                  pl.BlockSpec(mq.shape, const),
                  pl.BlockSpec(mk.shape, const),
                  pl.BlockSpec(bm.shape, lambda b, i: (0, 0, 0))],
        out_specs=pl.BlockSpec((1, T, nv), row),
        out_shape=jax.ShapeDtypeStruct((bn, L, nv), out_dtype),
        scratch_shapes=[pltpu.VMEM((H_C, DV_C, DK_C), F32)],
        compiler_params=_cparams("parallel", "arbitrary"),
        name="gla_mixer",
    )(qk, v, gc, rc, wa, b_alpha.reshape(1, nq), hnorm_g.reshape(1, nv), jnp.asarray(_tri(T)),
      jnp.asarray(mq), jnp.asarray(mk), jnp.asarray(bm))


def _rel_bucket(dist):
    n = jnp.maximum(dist, 0)
    max_exact = N_BUCKETS // 2
    nf = jnp.maximum(n, 1).astype(F32)
    large = max_exact + (jnp.log(nf / max_exact) / math.log(MAX_DIST / max_exact)
                         * (N_BUCKETS - max_exact)).astype(jnp.int32)
    large = jnp.minimum(large, N_BUCKETS - 1)
    return jnp.where(n < max_exact, n, large)


def _bias_table_kernel(bk_ref, rb_ref, o_ref):
    n = bk_ref.shape[0]
    onehot = (lax.broadcasted_iota(jnp.int32, (n, LANES), 1) == bk_ref[...]).astype(F32)
    o_ref[...] = _fdot(onehot, rb_ref[...])


def bias_table(rel_bias, L):
    bk = jnp.broadcast_to(_rel_bucket(jnp.arange(L, dtype=jnp.int32))[:, None], (L, LANES))
    rb = jnp.pad(rel_bias, ((0, LANES - rel_bias.shape[0]), (0, LANES - rel_bias.shape[1])))
    return pl.pallas_call(
        _bias_table_kernel,
        out_shape=jax.ShapeDtypeStruct((L, LANES), F32),
        name="bias_table",
    )(bk, rb)


def _dilated_kernel(q_ref, kp_ref, kc_ref, vp_ref, vc_ref, bias_ref, num_ref, m_ref, den_ref):
    W = q_ref.shape[1]
    first = pl.program_id(1) == 0
    col = lax.broadcasted_iota(jnp.int32, (1, 2 * W), 1)
    edge = jnp.where((col < W) & first, NEG_INF, 0.0).astype(F32)
    for h in range(H_D):
        cs = slice(h * DH_D, (h + 1) * DH_D)
        qh = q_ref[0, :, cs] * (DH_D ** -0.5)
        kcat = jnp.concatenate([kp_ref[0, :, cs], kc_ref[0, :, cs]], axis=0)
        vcat = jnp.concatenate([vp_ref[0, :, cs], vc_ref[0, :, cs]], axis=0)
        logits = _bdot_nt(qh, kcat) + bias_ref[h] + edge
        m = jnp.max(logits, axis=1, keepdims=True)
        p = jnp.exp(logits - m)
        num_ref[0, :, cs] = _bdot(p, vcat)
        m_ref[0, :, cs] = jnp.broadcast_to(m, (W, DH_D))
        den_ref[0, :, cs] = jnp.broadcast_to(jnp.sum(p, axis=1, keepdims=True), (W, DH_D))


def dilated_partial(q, k, v, table, window, dil):
    bn, L, nd = q.shape
    W = window // dil
    M = L // dil

    def to_sub(a):
        return a.reshape(bn, M, dil, nd).transpose(0, 2, 1, 3).reshape(bn * dil, M, nd)

    def from_sub(a):
        return a.reshape(bn, dil, M, nd).transpose(0, 2, 1, 3).reshape(bn, L, nd)

    tap = np.arange(W)[:, None] + W - np.arange(2 * W)[None, :]
    ok = (tap >= 0) & (tap <= W)
    bias = jnp.where(jnp.asarray(ok)[None], table[np.clip(tap, 0, W) * dil, :H_D].transpose(2, 0, 1), NEG_INF)
    qs, ks, vs = to_sub(q), to_sub(k), to_sub(v)
    cur = lambda i, j: (i, j, 0)
    prev = lambda i, j: (i, jnp.maximum(j - 1, 0), 0)
    blk = (1, W, nd)
    outs = pl.pallas_call(
        _dilated_kernel,
        grid=(bn * dil, M // W),
        in_specs=[pl.BlockSpec(blk, cur), pl.BlockSpec(blk, prev), pl.BlockSpec(blk, cur),
                  pl.BlockSpec(blk, prev), pl.BlockSpec(blk, cur),
                  pl.BlockSpec((H_D, W, 2 * W), lambda i, j: (0, 0, 0))],
        out_specs=[pl.BlockSpec(blk, cur)] * 3,
        out_shape=[jax.ShapeDtypeStruct(qs.shape, F32)] * 3,
        compiler_params=_cparams("parallel", "parallel"),
        name="dilated_partial",
    )(qs, ks, ks, vs, vs, bias)
    return [from_sub(a) for a in outs]


def _dilated_combine_kernel(*refs):
    o_ref = refs[-1]
    parts = [(refs[3 * i][0], refs[3 * i + 1][0], refs[3 * i + 2][0]) for i in range(len(DIL_PATTERNS))]
    m_all = functools.reduce(jnp.maximum, [p[1] for p in parts])
    ws = [jnp.exp(p[1] - m_all) for p in parts]
    num = sum(p[0] * w for p, w in zip(parts, ws))
    den = sum(p[2] * w for p, w in zip(parts, ws))
    o_ref[0] = (num / den).astype(o_ref.dtype)


def dilated_mixer(q, k, v, table, out_dtype=F32, tm=512):
    bn, L, nd = q.shape
    parts = []
    for window, dil in DIL_PATTERNS:
        parts += dilated_partial(q, k, v, table, window, dil)
    row = lambda b, i: (b, i, 0)
    return pl.pallas_call(
        _dilated_combine_kernel,
        grid=(bn, L // tm),
        in_specs=[pl.BlockSpec((1, tm, nd), row)] * len(parts),
        out_specs=pl.BlockSpec((1, tm, nd), row),
        out_shape=jax.ShapeDtypeStruct((bn, L, nd), out_dtype),
        compiler_params=_cparams("parallel", "parallel"),
        name="dilated_combine",
    )(*parts)


INT_MIN = -2 ** 31
DSA_TQ = 256
DSA_KC = 512


def _split_hi_lo(x):
    hi = x.astype(BF16)
    lo = (x - hi.astype(F32)).astype(BF16)
    return hi, lo


def _dsa_kernel(qb_ref, qi_ref, gt_ref, ki_ref, ckv_ref, wuk_ref, wuv_ref, frev_ref, o_ref,
                kcat, skey, thr_ref, j_ref, m_ref, l_ref, acc_ref, *, topk):
    tq = qb_ref.shape[1]
    L = ki_ref.shape[1]
    kc = DSA_KC
    i = pl.program_id(1)
    t0 = i * tq
    nck = (t0 + tq + kc - 1) // kc
    left = lax.broadcasted_iota(jnp.int32, (1, LANES), 1) < (LANES // 2)
    tpos = t0 + lax.broadcasted_iota(jnp.int32, (tq, 1), 0)
    lane_k = lax.broadcasted_iota(jnp.int32, (1, kc), 1)

    @pl.when(i == 0)
    def _():
        hi, lo = _split_hi_lo(ki_ref[0])
        kcat[:, :LANES] = hi
        kcat[:, LANES:] = jnp.where(left, lo, jnp.zeros_like(lo))

    qcats, ws = [], []
    for h in range(H_IDX):
        hi, lo = _split_hi_lo(qi_ref[0, :, h * LANES:(h + 1) * LANES])
        qcats.append(jnp.concatenate([jnp.where(left, hi, lo), jnp.where(left, hi, jnp.zeros_like(hi))], axis=1))
        ws.append(gt_ref[0, :, 2 * H_A + h:2 * H_A + h + 1] * (H_IDX ** -0.5))

    def score_body(c, carry):
        s0 = pl.multiple_of(c * kc, kc)
        kblk = kcat[pl.ds(s0, kc), :]
        sc = jnp.zeros((tq, kc), F32)
        for h in range(H_IDX):
            sc = sc + ws[h] * jnp.maximum(lax.dot_general(qcats[h], kblk, (((1,), (1,)), ((), ())),
                                                          preferred_element_type=F32), 0.0)
        bits = lax.bitcast_convert_type(sc + 0.0, jnp.int32)
        key = jnp.where(bits < 0, bits ^ jnp.int32(0x7FFFFFFF), bits)
        skey[:, pl.ds(s0, kc)] = jnp.where(s0 + lane_k <= tpos, key, jnp.int32(INT_MIN))
        return carry

    lax.fori_loop(0, nck, score_body, 0)

    def count(pred):
        def body(c, acc):
            s0 = pl.multiple_of(c * kc, kc)
            hit = pred(skey[:, pl.ds(s0, kc)], s0 + lane_k).astype(jnp.int32)
            return acc + functools.reduce(lambda a, b: a + b,
                                          [hit[:, a * LANES:(a + 1) * LANES] for a in range(kc // LANES)])
        acc = lax.fori_loop(0, nck, body, jnp.zeros((tq, LANES), jnp.int32))
        return jnp.sum(acc, axis=1, keepdims=True)

    thr_ref[...] = jnp.full(thr_ref.shape, INT_MIN, jnp.int32)
    j_ref[...] = jnp.full(j_ref.shape, L, jnp.int32)

    @pl.when(t0 + tq > topk)
    def _():
        def bit_body(it, tu):
            cand_u = tu | (jnp.int32(1) << (31 - it))
            cand = cand_u ^ jnp.int32(INT_MIN)
            return jnp.where(count(lambda blk, idx: blk >= cand) >= topk, cand_u, tu)

        tu = lax.fori_loop(0, 32, bit_body, jnp.zeros((tq, 1), jnp.int32))
        thr = tu ^ jnp.int32(INT_MIN)
        thr_ref[...] = jnp.broadcast_to(thr, thr_ref.shape)
        need = topk - count(lambda blk, idx: blk > thr)
        n_eq = count(lambda blk, idx: blk == thr)
        tied = ((n_eq > need) & (thr != INT_MIN)).astype(jnp.int32)

        @pl.when(jnp.max(tied) > 0)
        def _():
            nbits = max(1, (L - 1).bit_length())

            def jbit_body(it, jv):
                cand = jv | (jnp.int32(1) << (nbits - 1 - it))
                return jnp.where(count(lambda blk, idx: (blk == thr) & (idx < cand)) < need, cand, jv)

            jv = lax.fori_loop(0, nbits, jbit_body, jnp.zeros((tq, 1), jnp.int32))
            j_ref[...] = jnp.broadcast_to(jv, j_ref.shape)

    thr = thr_ref[:, 0:1]
    jv = j_ref[:, 0:1]
    qabs = [(_bdot_nt(qb_ref[0, :, h * DH_B:(h + 1) * DH_B], wuk_ref[h]) * (DH_B ** -0.5)).astype(BF16)
            for h in range(H_B)]
    m_ref[...] = jnp.full(m_ref.shape, NEG_INF, F32)
    l_ref[...] = jnp.zeros(l_ref.shape, F32)
    acc_ref[...] = jnp.zeros(acc_ref.shape, F32)
    wd = tq + kc

    def attn_body(c, carry):
        s0 = pl.multiple_of(c * kc, kc)
        blk = skey[:, pl.ds(s0, kc)]
        idx = s0 + lane_k
        sel = ((blk > thr) | ((blk == thr) & (idx <= jv))) & (idx <= tpos)
        madd = jnp.where(sel, 0.0, NEG_INF).astype(F32)
        ck = ckv_ref[0, pl.ds(s0, kc), :]
        off = pl.multiple_of(L - (t0 - s0) - tq, LANES)
        for h in range(H_B):
            x = jnp.broadcast_to(frev_ref[h:h + 1, pl.ds(off, wd)], (tq, wd))
            bias = pltpu.roll(x, kc, 1, stride=1, stride_axis=0)[:, :kc]
            logits = _bdot_nt(qabs[h], ck) + bias + madd
            m_old = m_ref[h][:, 0:1]
            m_new = jnp.maximum(m_old, jnp.max(logits, axis=1, keepdims=True))
            alpha = jnp.exp(m_old - m_new)
            p = jnp.exp(logits - m_new)
            l_ref[h] = alpha * l_ref[h] + jnp.sum(p, axis=1, keepdims=True)
            acc_ref[h] = alpha * acc_ref[h] + _bdot(p, ck)
            m_ref[h] = jnp.broadcast_to(m_new, m_ref.shape[1:])
        return carry

    lax.fori_loop(0, nck, attn_body, 0)
    for h in range(H_B):
        o_lat = acc_ref[h] / l_ref[h][:, 0:1]
        o_ref[0, :, h * DH_B:(h + 1) * DH_B] = _bdot(o_lat, wuv_ref[h]).astype(o_ref.dtype)


def dsa_mixer(qb, qi, gates, ki, ckv, w_uk, w_uv, table, out_dtype=F32):
    bn, L, _ = qb.shape
    tq = min(DSA_TQ, L)
    topk = min(TOPK_MAX, L // TOPK_DIV)
    wuk = jnp.transpose(w_uk, (1, 0, 2)).astype(BF16)
    wuv = jnp.transpose(w_uv, (1, 0, 2)).astype(BF16)
    tt = table[:, :8].T
    frev = jnp.concatenate([jnp.zeros((8, 1), F32), tt[:, ::-1], jnp.zeros((8, DSA_KC - 1), F32)], axis=1)
    row = lambda b, i: (b, i, 0)
    full = lambda b, i: (b, 0, 0)
    c3 = lambda b, i: (0, 0, 0)
    kern = functools.partial(_dsa_kernel, topk=topk)
    return pl.pallas_call(
        kern,
        grid=(bn, L // tq),
        in_specs=[pl.BlockSpec((1, tq, H_B * DH_B), row),
                  pl.BlockSpec((1, tq, H_IDX * LANES), row),
                  pl.BlockSpec((1, tq, LANES), row),
                  pl.BlockSpec((1, L, LANES), full),
                  pl.BlockSpec((1, L, R_KV), full),
                  pl.BlockSpec(wuk.shape, c3),
                  pl.BlockSpec(wuv.shape, c3),
                  pl.BlockSpec(frev.shape, lambda b, i: (0, 0))],
        out_specs=pl.BlockSpec((1, tq, H_B * DH_B), row),
        out_shape=jax.ShapeDtypeStruct((bn, L, H_B * DH_B), out_dtype),
        scratch_shapes=[pltpu.VMEM((L, 2 * LANES), BF16),
                        pltpu.VMEM((tq, L), jnp.int32),
                        pltpu.VMEM((tq, LANES), jnp.int32),
                        pltpu.VMEM((tq, LANES), jnp.int32),
                        pltpu.VMEM((H_B, tq, LANES), F32),
                        pltpu.VMEM((H_B, tq, LANES), F32),
                        pltpu.VMEM((H_B, tq, R_KV), F32)],
        compiler_params=_cparams("parallel", "arbitrary"),
        name="dsa_mixer",
    )(qb, qi, gates, ki, ckv, wuk, wuv, frev)


def _router_gates(lg):
    lane = lax.broadcasted_iota(jnp.int32, lg.shape, 1)
    ninf = jnp.float32(-jnp.inf)
    lc = jnp.where(lane < N_GROUPS, lg, ninf)
    mc = jnp.max(lc, axis=1, keepdims=True)
    gidx = jnp.min(jnp.where(lc == mc, lane, LANES), axis=1, keepdims=True)
    p_g = 1.0 / jnp.sum(jnp.exp(lc - mc), axis=1, keepdims=True)
    lo = N_GROUPS + E_PER_GROUP * gidx
    lf = jnp.where((lane >= lo) & (lane < lo + E_PER_GROUP), lg, ninf)
    v1 = jnp.max(lf, axis=1, keepdims=True)
    i1 = jnp.min(jnp.where(lf == v1, lane, LANES), axis=1, keepdims=True)
    lf2 = jnp.where(lane == i1, ninf, lf)
    v2 = jnp.max(lf2, axis=1, keepdims=True)
    i2 = jnp.min(jnp.where(lf2 == v2, lane, LANES), axis=1, keepdims=True)
    e2 = jnp.exp(v2 - v1)
    p1 = 1.0 / (1.0 + e2)
    p2 = e2 / (1.0 + e2)
    return p_g * jnp.where(lane == i1, p1, jnp.where(lane == i2, p2, 0.0))


def _moe_dense_kernel(x_ref, g_ref, sc_ref, sh_ref, gt_ref, wr_ref, br_ref, wg_ref, wu_ref, wd_ref, gf_ref,
                      o_ref, hf, gates, acc, *, final):
    e = pl.program_id(2)

    @pl.when(e == 0)
    def _():
        h = _modulated_norm(x_ref[0], g_ref[...], sc_ref[0], sh_ref[0])
        hf[...] = h.astype(BF16)
        gates[...] = _router_gates(_fdot(h, wr_ref[...]) + br_ref[...])
        acc[...] = jnp.zeros(acc.shape, F32)

    lane = lax.broadcasted_iota(jnp.int32, gates.shape, 1)
    gcol = jnp.sum(jnp.where(lane == N_GROUPS + e, gates[...], 0.0), axis=1, keepdims=True)
    hb = hf[...]
    a = jnp.dot(hb, wg_ref[0], preferred_element_type=F32)
    a = a * jax.nn.sigmoid(a) * jnp.dot(hb, wu_ref[0], preferred_element_type=F32) * gcol
    acc[...] += _bdot(a, wd_ref[0])

    @pl.when(e == pl.num_programs(2) - 1)
    def _():
        y = x_ref[0] + gt_ref[0] * acc[...]
        if final:
            y = y * lax.rsqrt(jnp.mean(y * y, axis=-1, keepdims=True) + EPS) * gf_ref[...]
        o_ref[0] = y


def moe_layer(x, g, sc, sh, gt, w_coarse, b_coarse, w_fine, b_fine, w_gate, w_up, w_down, g_final=None, tm=1024):
    bn, L, d = x.shape
    ne = N_GROUPS * E_PER_GROUP
    ff = w_gate.shape[-1]
    wr = jnp.concatenate([w_coarse, jnp.transpose(w_fine, (1, 0, 2)).reshape(d, ne)], axis=1)
    wr = _pad_cols(wr, LANES)
    br = _pad_cols(jnp.concatenate([b_coarse, b_fine.reshape(-1)])[None, :], LANES)
    wg = w_gate.reshape(ne, d, ff).astype(BF16)
    wu = w_up.reshape(ne, d, ff).astype(BF16)
    wd = w_down.reshape(ne, ff, d).astype(BF16)
    final = g_final is not None
    gf = (g_final if final else jnp.ones((d,), F32)).reshape(1, d)
    row = lambda b, i, e: (b, i, 0)
    per_b = lambda b, i, e: (b, 0, 0)
    const = lambda b, i, e: (0, 0)
    per_e = lambda b, i, e: (e, 0, 0)
    kern = functools.partial(_moe_dense_kernel, final=final)
    return pl.pallas_call(
        kern,
        grid=(bn, L // tm, ne),
        in_specs=[pl.BlockSpec((1, tm, d), row),
                  pl.BlockSpec((1, d), const),
                  pl.BlockSpec((1, 1, d), per_b),
                  pl.BlockSpec((1, 1, d), per_b),
                  pl.BlockSpec((1, 1, d), per_b),
                  pl.BlockSpec((d, LANES), const),
                  pl.BlockSpec((1, LANES), const),
                  pl.BlockSpec((1, d, ff), per_e),
                  pl.BlockSpec((1, d, ff), per_e),
                  pl.BlockSpec((1, ff, d), per_e),
                  pl.BlockSpec((1, d), const)],
        out_specs=pl.BlockSpec((1, tm, d), row),
        out_shape=jax.ShapeDtypeStruct((bn, L, d), F32),
        scratch_shapes=[pltpu.VMEM((tm, d), BF16), pltpu.VMEM((tm, LANES), F32), pltpu.VMEM((tm, d), F32)],
        compiler_params=_cparams("parallel", "parallel", "arbitrary"),
        name="moe_layer",
    )(x, g.reshape(1, d), sc.reshape(bn, 1, d), sh.reshape(bn, 1, d), gt.reshape(bn, 1, d), wr, br, wg, wu, wd, gf)


def _pad_cols(w, width):
    return jnp.pad(w, ((0, 0), (0, width - w.shape[1])))


def _segments(widths):
    offs = np.cumsum((0,) + tuple(widths))
    return tuple((int(o), int(w)) for o, w in zip(offs[:-1], widths))


AB_WIDTHS = (2 * H_A * DK_A, H_A * DV_A, H_A * DV_A, LANES, H_B * DH_B, R_KV, H_IDX * LANES, LANES)
AB_DTYPES = (F32, F32, F32, F32, F32, BF16, F32, F32)
CD_WIDTHS = (2 * H_C * DK_C, H_C * DV_C, H_C * DV_C, LANES, H_D * DH_D, H_D * DH_D, H_D * DH_D)
CD_DTYPES = (F32,) * 7


def pack_ab_w_in(w):
    d = w.shape[0]
    nqk = 2 * H_A * DK_A
    nv = H_A * DV_A
    o_i = nqk + 2 * nv
    o_qb = o_i + 2 * H_A
    o_ckv = o_qb + H_B * DH_B
    o_qi = o_ckv + R_KV
    o_ki = o_qi + H_IDX * D_IDX
    o_wi = o_ki + D_IDX
    gates = _pad_cols(jnp.concatenate([w[:, o_i:o_qb], w[:, o_wi:o_wi + H_IDX]], axis=1), LANES)
    wqi = w[:, o_qi:o_ki].reshape(d, H_IDX, D_IDX)
    wqi = jnp.concatenate([wqi, wqi], axis=-1).reshape(d, H_IDX * LANES)
    wki = w[:, o_ki:o_wi]
    return jnp.concatenate([w[:, :o_i], gates, w[:, o_qb:o_ckv], w[:, o_ckv:o_qi], wqi, wki, wki],
                           axis=1).astype(BF16)


def pack_cd_w_in(w):
    o_g = 2 * H_C * DK_C + H_C * DV_C
    o_r = o_g + GLA_RANK
    o_q = o_r + H_C * DV_C
    return jnp.concatenate([w[:, :o_g], w[:, o_r:o_q], _pad_cols(w[:, o_g:o_r], LANES), w[:, o_q:]],
                           axis=1).astype(BF16)


def kernel(x, c, w_ada, b_ada, g_mix, g_ffn, g_final, rel_bias, ab_w_in, ab_conv_w, ab_conv_b, ab_gate_b, ab_hnorm_g, ab_w_uk, ab_w_uv, ab_w_out, cd_w_in, cd_w_alpha, cd_b_alpha, cd_hnorm_g, cd_w_out, moe_w_coarse, moe_b_coarse, moe_w_fine, moe_b_fine, moe_w_gate, moe_w_up, moe_w_down):
    depth = w_ada.shape[0]
    L, d = x.shape[1], x.shape[2]
    mod = ada_mod(c, w_ada, b_ada)
    table = bias_table(rel_bias, L)
    for l in range(depth):
        sh_m, sc_m, gt_m, sh_f, sc_f, gt_f = [mod[l, :, i * d:(i + 1) * d] for i in range(6)]
        j = l // 2
        if l % 2 == 0:
            qk, va, oa, gates, qb, ckv, qi, ki = norm_proj(
                x, g_mix[l], sc_m, sh_m, pack_ab_w_in(ab_w_in[j]), _segments(AB_WIDTHS), AB_DTYPES)
            y1 = mlstm_mixer(qk, va, oa, gates, ab_conv_w[j], ab_conv_b[j], ab_gate_b[j], ab_hnorm_g[j])
            y2 = dsa_mixer(qb, qi, gates, ki, ckv, ab_w_uk[j], ab_w_uv[j], table)
            w_out = ab_w_out[j]
        else:
            qk, vc, rc, gc, qd, kd, vd = norm_proj(
                x, g_mix[l], sc_m, sh_m, pack_cd_w_in(cd_w_in[j]), _segments(CD_WIDTHS), CD_DTYPES)
            y1 = gla_mixer(qk, vc, gc, rc, cd_w_alpha[j], cd_b_alpha[j], cd_hnorm_g[j])
            y2 = dilated_mixer(qd, kd, vd, table)
            w_out = cd_w_out[j]
        x = outproj_residual(x, y1, y2, gt_m, w_out.astype(BF16))
        x = moe_layer(x, g_ffn[l], sc_f, sh_f, gt_f, moe_w_coarse[l], moe_b_coarse[l], moe_w_fine[l],
                      moe_b_fine[l], moe_w_gate[l], moe_w_up[l], moe_w_down[l],
                      g_final=g_final if l == depth - 1 else None)
    return x
```

```python
import functools
import math

import numpy as np
import jax
import jax.numpy as jnp
from jax import lax
from jax.experimental import pallas as pl
from jax.experimental.pallas import tpu as pltpu

F32 = jnp.float32
BF16 = jnp.bfloat16

EPS = 1e-6
NEG_INF = -1e30

H_A, DK_A, DV_A, CONV_A, MLSTM_CHUNK = 4, 64, 128, 4, 64
H_B, DH_B, R_KV, H_IDX, D_IDX = 8, 64, 128, 4, 64
TOPK_MAX, TOPK_DIV = 256, 4
H_C, DK_C, DV_C, GLA_RANK, GLA_TAU, GLA_CHUNK = 4, 64, 128, 16, 16.0, 64
H_D, DH_D = 8, 64
DIL_PATTERNS = ((128, 1), (512, 4), (2048, 16))
N_BUCKETS, MAX_DIST = 32, 2048
N_GROUPS, E_PER_GROUP, EXPERT_FF = 4, 4, 512

LANES = 128
VMEM_LIMIT = 56 * 1024 * 1024


def _cparams(*sem):
    return pltpu.CompilerParams(dimension_semantics=sem, vmem_limit_bytes=VMEM_LIMIT)


def _bdot(a, b):
    return jnp.dot(a.astype(BF16), b.astype(BF16), preferred_element_type=F32)


def _bdot_nt(a, b):
    return lax.dot_general(a.astype(BF16), b.astype(BF16), (((1,), (1,)), ((), ())),
                           preferred_element_type=F32)


def _fdot(a, b):
    return jnp.dot(a, b, preferred_element_type=F32, precision=lax.Precision.HIGHEST)


def _ada_kernel(c_ref, w_ref, b_ref, o_ref):
    c = c_ref[...]
    a = c * jax.nn.sigmoid(c)
    o_ref[0] = _fdot(a, w_ref[0]) + b_ref[0]


def ada_mod(c, w_ada, b_ada):
    depth, d, e = w_ada.shape
    bn = c.shape[0]
    tn = 1536
    return pl.pallas_call(
        _ada_kernel,
        grid=(depth, e // tn),
        in_specs=[pl.BlockSpec((bn, d), lambda l, j: (0, 0)),
                  pl.BlockSpec((1, d, tn), lambda l, j: (l, 0, j)),
                  pl.BlockSpec((1, 1, tn), lambda l, j: (l, 0, j))],
        out_specs=pl.BlockSpec((1, bn, tn), lambda l, j: (l, 0, j)),
        out_shape=jax.ShapeDtypeStruct((depth, bn, e), F32),
        compiler_params=_cparams("arbitrary", "arbitrary"),
        name="ada_mod",
    )(c, w_ada, b_ada.reshape(depth, 1, e))


def _modulated_norm(x, g, sc, sh):
    y = x * lax.rsqrt(jnp.mean(x * x, axis=-1, keepdims=True) + EPS) * g
    return y * (1.0 + sc) + sh


def _norm_proj_kernel(x_ref, g_ref, sc_ref, sh_ref, w_ref, *o_refs, segs):
    h = _modulated_norm(x_ref[0], g_ref[...], sc_ref[0], sh_ref[0]).astype(BF16)
    for o_ref, (start, width) in zip(o_refs, segs):
        o_ref[0] = jnp.dot(h, w_ref[:, start:start + width], preferred_element_type=F32).astype(o_ref.dtype)


def norm_proj(x, g, sc, sh, w, segs, dtypes, tm=512):
    bn, L, d = x.shape
    wp = w.shape[1]
    kern = functools.partial(_norm_proj_kernel, segs=tuple(segs))
    return pl.pallas_call(
        kern,
        grid=(bn, L // tm),
        in_specs=[pl.BlockSpec((1, tm, d), lambda b, i: (b, i, 0)),
                  pl.BlockSpec((1, d), lambda b, i: (0, 0)),
                  pl.BlockSpec((1, 1, d), lambda b, i: (b, 0, 0)),
                  pl.BlockSpec((1, 1, d), lambda b, i: (b, 0, 0)),
                  pl.BlockSpec((d, wp), lambda b, i: (0, 0))],
        out_specs=[pl.BlockSpec((1, tm, wd), lambda b, i: (b, i, 0)) for (_, wd) in segs],
        out_shape=[jax.ShapeDtypeStruct((bn, L, wd), dt) for (_, wd), dt in zip(segs, dtypes)],
        compiler_params=_cparams("parallel", "parallel"),
        name="norm_proj",
    )(x, g.reshape(1, d), sc.reshape(bn, 1, d), sh.reshape(bn, 1, d), w)


def _outproj_kernel(x_ref, ya_ref, yb_ref, gt_ref, w_ref, o_ref):
    na = ya_ref.shape[-1]
    y = _bdot(ya_ref[0], w_ref[:na, :]) + _bdot(yb_ref[0], w_ref[na:, :])
    o_ref[0] = x_ref[0] + gt_ref[0] * y


def outproj_residual(x, ya, yb, gt, w, tm=512):
    bn, L, d = x.shape
    na, nb = ya.shape[-1], yb.shape[-1]
    return pl.pallas_call(
        _outproj_kernel,
        grid=(bn, L // tm),
        in_specs=[pl.BlockSpec((1, tm, d), lambda b, i: (b, i, 0)),
                  pl.BlockSpec((1, tm, na), lambda b, i: (b, i, 0)),
                  pl.BlockSpec((1, tm, nb), lambda b, i: (b, i, 0)),
                  pl.BlockSpec((1, 1, d), lambda b, i: (b, 0, 0)),
                  pl.BlockSpec((na + nb, d), lambda b, i: (0, 0))],
        out_specs=pl.BlockSpec((1, tm, d), lambda b, i: (b, i, 0)),
        out_shape=jax.ShapeDtypeStruct((bn, L, d), F32),
        compiler_params=_cparams("parallel", "parallel"),
        name="outproj_residual",
    )(x, ya, yb, gt.reshape(bn, 1, d), w)


def _log_sigmoid(x):
    return jnp.minimum(x, 0.0) - jnp.log1p(jnp.exp(-jnp.abs(x)))


def _dot_tn(a, b):
    return lax.dot_general(a.astype(BF16), b.astype(BF16), (((0,), (0,)), ((), ())),
                           preferred_element_type=F32)


def _tri(T):
    r = np.arange(T)
    return (r[:, None] >= r[None, :]).astype(np.float32)


def _mlstm_kernel(qk_ref, v_ref, og_ref, gt_ref, cw_ref, cb_ref, gb_ref, hg_ref, ltri_ref, utri_ref,
                  o_ref, xbuf, cext, mstate):
    T = qk_ref.shape[1]
    tail = xbuf.shape[0] - T

    @pl.when(pl.program_id(1) == 0)
    def _():
        xbuf[0:tail, :] = jnp.zeros((tail, xbuf.shape[1]), F32)
        cext[...] = jnp.zeros(cext.shape, F32)
        mstate[...] = jnp.full(mstate.shape, NEG_INF, F32)

    x = qk_ref[0]
    xbuf[tail:tail + T, :] = x
    conv = cb_ref[...] + sum(xbuf[pl.ds(tail - (CONV_A - 1) + j, T), :] * cw_ref[j:j + 1, :]
                             for j in range(CONV_A))
    xbuf[0:tail, :] = x[T - tail:T, :]
    qk = conv * jax.nn.sigmoid(conv)

    G = gt_ref[0] + gb_ref[...]
    GT = G.T
    bcol = _fdot(ltri_ref[...], _log_sigmoid(G))
    brow = _fdot(_log_sigmoid(GT), utri_ref[...])
    causal = ltri_ref[...] > 0.5
    lane = lax.broadcasted_iota(jnp.int32, (T, DV_A), 1)
    ones_blk = jnp.where(lane == 0, 1.0, 0.0).astype(F32)
    nq = H_A * DK_A

    for h in range(H_A):
        b_c = bcol[:, H_A + h:H_A + h + 1]
        b_r = brow[H_A + h:H_A + h + 1, :]
        li_r = GT[h:h + 1, :]
        li_c = G[:, h:h + 1]
        m_prev = mstate[h:h + 1, 0:1]
        Dm = jnp.where(causal, b_c - b_r + li_r, NEG_INF)
        g = b_c + m_prev
        mt = jnp.maximum(g, jnp.max(Dm, axis=1, keepdims=True))
        w_intra = jnp.exp(Dm - mt)
        w_state = jnp.exp(g - mt)
        q = qk[:, h * DK_A:(h + 1) * DK_A] * (DK_A ** -0.5)
        k = qk[:, nq + h * DK_A:nq + (h + 1) * DK_A]
        vext = jnp.concatenate([v_ref[0, :, h * DV_A:(h + 1) * DV_A], ones_blk], axis=1)
        s = _bdot_nt(q, k) * w_intra
        nd = w_state * _bdot(q, cext[h]) + _bdot(s, vext)
        num = nd[:, :DV_A]
        den = nd[:, DV_A:DV_A + 1]
        hout = num / jnp.maximum(jnp.abs(den), jnp.exp(-mt))
        bL = b_c[T - 1:T, :]
        d_end = bL - b_c + li_c
        m_new = jnp.maximum(bL + m_prev, jnp.max(d_end, axis=0, keepdims=True))
        w_e = jnp.exp(d_end - m_new)
        decay = jnp.exp(bL + m_prev - m_new)
        cext[h] = decay * cext[h] + _dot_tn(k * w_e, vext)
        mstate[h:h + 1, :] = jnp.broadcast_to(m_new, (1, mstate.shape[1]))
        y = hout * lax.rsqrt(jnp.mean(hout * hout, axis=-1, keepdims=True) + EPS)
        y = y * hg_ref[:, h * DV_A:(h + 1) * DV_A]
        o_ref[0, :, h * DV_A:(h + 1) * DV_A] = (
            y * jax.nn.sigmoid(og_ref[0, :, h * DV_A:(h + 1) * DV_A])).astype(o_ref.dtype)


def mlstm_mixer(qk, v, og, gates, conv_w, conv_b, gate_b, hnorm_g, out_dtype=F32):
    bn, L, _ = qk.shape
    T = MLSTM_CHUNK
    nv = H_A * DV_A
    gb = jnp.zeros((1, LANES), F32).at[0, :2 * H_A].set(gate_b.reshape(-1))
    ltri = jnp.asarray(_tri(T))
    row = lambda b, i: (b, i, 0)
    const = lambda b, i: (0, 0)
    return pl.pallas_call(
        _mlstm_kernel,
        grid=(bn, L // T),
        in_specs=[pl.BlockSpec((1, T, qk.shape[-1]), row),
                  pl.BlockSpec((1, T, nv), row),
                  pl.BlockSpec((1, T, nv), row),
                  pl.BlockSpec((1, T, LANES), row),
                  pl.BlockSpec(conv_w.shape, const),
                  pl.BlockSpec((1, conv_b.shape[-1]), const),
                  pl.BlockSpec((1, LANES), const),
                  pl.BlockSpec((1, nv), const),
                  pl.BlockSpec((T, T), const),
                  pl.BlockSpec((T, T), const)],
        out_specs=pl.BlockSpec((1, T, nv), row),
        out_shape=jax.ShapeDtypeStruct((bn, L, nv), out_dtype),
        scratch_shapes=[pltpu.VMEM((8 + T, qk.shape[-1]), F32),
                        pltpu.VMEM((H_A, DK_A, 2 * DV_A), F32),
                        pltpu.VMEM((8, LANES), F32)],
        compiler_params=_cparams("parallel", "arbitrary"),
        name="mlstm_mixer",
    )(qk, v, og, gates, conv_w, conv_b.reshape(1, -1), gb, hnorm_g.reshape(1, -1), ltri, ltri.T)


def _gla_level_consts(T):
    r = np.arange(T)
    mq, mk, bm = [], [], []
    n = T
    while n >= 2:
        start = (r // n) * n
        mid = start + n // 2
        upper = r >= mid
        mq.append((upper[:, None] & (r[None, :] >= mid[:, None]) & (r[None, :] <= r[:, None])).astype(np.float32))
        mk.append((~upper[:, None] & (r[None, :] > r[:, None]) & (r[None, :] < mid[:, None])).astype(np.float32))
        bm.append((upper[:, None] & ~upper[None, :] & (start[:, None] == start[None, :])).astype(np.float32))
        n //= 2
    bm.append(np.eye(T, dtype=np.float32))
    return np.concatenate(mq, 0), np.concatenate(mk, 0), np.stack(bm, 0)


def _gla_kernel(qk_ref, v_ref, gc_ref, rc_ref, wa_ref, ba_ref, hg_ref, ltri_ref, mq_ref, mk_ref, bm_ref,
                o_ref, st):
    T = qk_ref.shape[1]
    nlev = bm_ref.shape[0] - 1
    nq = H_C * DK_C

    @pl.when(pl.program_id(1) == 0)
    def _():
        st[...] = jnp.zeros(st.shape, F32)

    qk = qk_ref[0]
    q = qk[:, :nq] * (DK_C ** -0.5)
    k = qk[:, nq:]
    la = _log_sigmoid(_bdot(gc_ref[0], wa_ref[...]) + ba_ref[...]) * (1.0 / GLA_TAU)
    eq = jnp.exp(_fdot(mq_ref[...], la))
    ek = jnp.exp(_fdot(mk_ref[...], la))
    bfull = _fdot(ltri_ref[...], la)
    bL = bfull[T - 1:T, :]
    qe = q * jnp.exp(bfull)
    kd = k * jnp.exp(bL - bfull)
    eL = jnp.exp(bL)

    for h in range(H_C):
        cs = slice(h * DK_C, (h + 1) * DK_C)
        vs = slice(h * DV_C, (h + 1) * DV_C)
        qh, kh, vh = q[:, cs], k[:, cs], v_ref[0, :, vs]
        A = bm_ref[nlev] * _bdot_nt(qh, kh)
        for lv in range(nlev):
            A = A + bm_ref[lv] * _bdot_nt(qh * eq[lv * T:(lv + 1) * T, cs], kh * ek[lv * T:(lv + 1) * T, cs])
        o = _bdot_nt(qe[:, cs], st[h]) + _bdot(A, vh)
        st[h] = eL[:, cs] * st[h] + _dot_tn(vh, kd[:, cs])
        y = o * lax.rsqrt(jnp.mean(o * o, axis=-1, keepdims=True) + EPS) * hg_ref[:, vs]
        r = rc_ref[0, :, vs]
        o_ref[0, :, vs] = (y * (r * jax.nn.sigmoid(r))).astype(o_ref.dtype)


def gla_mixer(qk, v, gc, rc, w_alpha, b_alpha, hnorm_g, out_dtype=F32):
    bn, L, _ = qk.shape
    T = GLA_CHUNK
    nv = H_C * DV_C
    nq = H_C * DK_C
    wa = jnp.pad(w_alpha, ((0, gc.shape[-1] - w_alpha.shape[0]), (0, 0))).astype(BF16)
    mq, mk, bm = _gla_level_consts(T)
    row = lambda b, i: (b, i, 0)
    const = lambda b, i: (0, 0)
    return pl.pallas_call(
        _gla_kernel,
        grid=(bn, L // T),
        in_specs=[pl.BlockSpec((1, T, 2 * nq), row),
                  pl.BlockSpec((1, T, nv), row),
                  pl.BlockSpec((1, T, gc.shape[-1]), row),
                  pl.BlockSpec((1, T, nv), row),
                  pl.BlockSpec(wa.shape, const),
                  pl.BlockSpec((1, nq), const),
                  pl.BlockSpec((1, nv), const),
                  pl.BlockSpec((T, T), const),
                  pl.BlockSpec(mq.shape, const),
                  pl.BlockSpec(mk.shape, const),
                  pl.BlockSpec(bm.shape, lambda b, i: (0, 0, 0))],
        out_specs=pl.BlockSpec((1, T, nv), row),
        out_shape=jax.ShapeDtypeStruct((bn, L, nv), out_dtype),
        scratch_shapes=[pltpu.VMEM((H_C, DV_C, DK_C), F32)],
        compiler_params=_cparams("parallel", "arbitrary"),
        name="gla_mixer",
    )(qk, v, gc, rc, wa, b_alpha.reshape(1, nq), hnorm_g.reshape(1, nv), jnp.asarray(_tri(T)),
      jnp.asarray(mq), jnp.asarray(mk), jnp.asarray(bm))


def _rel_bucket(dist):
    n = jnp.maximum(dist, 0)
    max_exact = N_BUCKETS // 2
    nf = jnp.maximum(n, 1).astype(F32)
    large = max_exact + (jnp.log(nf / max_exact) / math.log(MAX_DIST / max_exact)
                         * (N_BUCKETS - max_exact)).astype(jnp.int32)
    large = jnp.minimum(large, N_BUCKETS - 1)
    return jnp.where(n < max_exact, n, large)


def _bias_lookup_kernel(bk_ref, rb_ref, o_ref):
    n = bk_ref.shape[0]
    onehot = (lax.broadcasted_iota(jnp.int32, (n, LANES), 1) == bk_ref[...]).astype(F32)
    o_ref[...] = _fdot(onehot, rb_ref[...])


def bias_lookup(rel_bias, dist):
    n = dist.shape[0]
    tn = min(n, 4096)
    bk = jnp.broadcast_to(_rel_bucket(dist)[:, None], (n, LANES))
    rb = jnp.pad(rel_bias, ((0, LANES - rel_bias.shape[0]), (0, LANES - rel_bias.shape[1])))
    return pl.pallas_call(
        _bias_lookup_kernel,
        grid=(n // tn,),
        in_specs=[pl.BlockSpec((tn, LANES), lambda i: (i, 0)), pl.BlockSpec((LANES, LANES), lambda i: (0, 0))],
        out_specs=pl.BlockSpec((tn, LANES), lambda i: (i, 0)),
        out_shape=jax.ShapeDtypeStruct((n, LANES), F32),
        compiler_params=_cparams("parallel"),
        name="bias_lookup",
    )(bk, rb)


def bias_table(rel_bias, L):
    return bias_lookup(rel_bias, jnp.arange(L, dtype=jnp.int32))


def _dilated_kernel(q_ref, kp_ref, kc_ref, vp_ref, vc_ref, bias_ref, num_ref, m_ref, den_ref):
    W = q_ref.shape[1]
    first = pl.program_id(1) == 0
    col = lax.broadcasted_iota(jnp.int32, (1, 2 * W), 1)
    edge = jnp.where((col < W) & first, NEG_INF, 0.0).astype(F32)
    for h in range(H_D):
        cs = slice(h * DH_D, (h + 1) * DH_D)
        qh = q_ref[0, :, cs] * (DH_D ** -0.5)
        kcat = jnp.concatenate([kp_ref[0, :, cs], kc_ref[0, :, cs]], axis=0)
        vcat = jnp.concatenate([vp_ref[0, :, cs], vc_ref[0, :, cs]], axis=0)
        logits = _bdot_nt(qh, kcat) + bias_ref[h] + edge
        m = jnp.max(logits, axis=1, keepdims=True)
        p = jnp.exp(logits - m)
        num_ref[0, :, cs] = _bdot(p, vcat)
        m_ref[0, :, cs] = jnp.broadcast_to(m, (W, DH_D))
        den_ref[0, :, cs] = jnp.broadcast_to(jnp.sum(p, axis=1, keepdims=True), (W, DH_D))


def dilated_partial(q, k, v, rel_bias, window, dil):
    bn, L, nd = q.shape
    W = window // dil
    M = L // dil

    def to_sub(a):
        return a.reshape(bn, M, dil, nd).transpose(0, 2, 1, 3).reshape(bn * dil, M, nd)

    def from_sub(a):
        return a.reshape(bn, dil, M, nd).transpose(0, 2, 1, 3).reshape(bn, L, nd)

    tap = np.arange(W)[:, None] + W - np.arange(2 * W)[None, :]
    ok = (tap >= 0) & (tap <= W)
    tile = bias_lookup(rel_bias, jnp.asarray((np.clip(tap, 0, W) * dil).reshape(-1), jnp.int32))
    bias = jnp.where(jnp.asarray(ok)[None], tile[:, :H_D].reshape(W, 2 * W, H_D).transpose(2, 0, 1), NEG_INF)
    qs, ks, vs = to_sub(q), to_sub(k), to_sub(v)
    cur = lambda i, j: (i, j, 0)
    prev = lambda i, j: (i, jnp.maximum(j - 1, 0), 0)
    blk = (1, W, nd)
    outs = pl.pallas_call(
        _dilated_kernel,
        grid=(bn * dil, M // W),
        in_specs=[pl.BlockSpec(blk, cur), pl.BlockSpec(blk, prev), pl.BlockSpec(blk, cur),
                  pl.BlockSpec(blk, prev), pl.BlockSpec(blk, cur),
                  pl.BlockSpec((H_D, W, 2 * W), lambda i, j: (0, 0, 0))],
        out_specs=[pl.BlockSpec(blk, cur)] * 3,
        out_shape=[jax.ShapeDtypeStruct(qs.shape, F32)] * 3,
        compiler_params=_cparams("parallel", "parallel"),
        name="dilated_partial",
    )(qs, ks, ks, vs, vs, bias)
    return [from_sub(a) for a in outs]


def _dilated_combine_kernel(*refs):
    o_ref = refs[-1]
    parts = [(refs[3 * i][0], refs[3 * i + 1][0], refs[3 * i + 2][0]) for i in range(len(DIL_PATTERNS))]
    m_all = functools.reduce(jnp.maximum, [p[1] for p in parts])
    ws = [jnp.exp(p[1] - m_all) for p in parts]
    num = sum(p[0] * w for p, w in zip(parts, ws))
    den = sum(p[2] * w for p, w in zip(parts, ws))
    o_ref[0] = (num / den).astype(o_ref.dtype)


def dilated_mixer(q, k, v, rel_bias, out_dtype=F32, tm=512):
    bn, L, nd = q.shape
    parts = []
    for window, dil in DIL_PATTERNS:
        parts += dilated_partial(q, k, v, rel_bias, window, dil)
    row = lambda b, i: (b, i, 0)
    return pl.pallas_call(
        _dilated_combine_kernel,
        grid=(bn, L // tm),
        in_specs=[pl.BlockSpec((1, tm, nd), row)] * len(parts),
        out_specs=pl.BlockSpec((1, tm, nd), row),
        out_shape=jax.ShapeDtypeStruct((bn, L, nd), out_dtype),
        compiler_params=_cparams("parallel", "parallel"),
        name="dilated_combine",
    )(*parts)


INT_MIN = -2 ** 31
DSA_TQ = 256
DSA_KC = 512
DSA_RB = 256


def _split_hi_lo(x):
    hi = x.astype(BF16)
    lo = (x - hi.astype(F32)).astype(BF16)
    return hi, lo


def _dsa_kernel(qb_ref, qi_ref, gt_ref, ki_ref, ckv_ref, wuk_ref, wuv_ref, frev_ref, o_ref,
                skew, kcat, cke, skey, thr_ref, j_ref, m_ref, acc_ref, *, topk):
    tq = qb_ref.shape[1]
    L = ki_ref.shape[1]
    kc = DSA_KC
    i = pl.program_id(1)
    t0 = i * tq
    nck = (t0 + tq + kc - 1) // kc
    left = lax.broadcasted_iota(jnp.int32, (1, LANES), 1) < (LANES // 2)
    tpos = t0 + lax.broadcasted_iota(jnp.int32, (tq, 1), 0)
    lane_k = lax.broadcasted_iota(jnp.int32, (1, kc), 1)

    @pl.when((pl.program_id(0) == 0) & (i == 0))
    def _():
        ltot = frev_ref.shape[1]
        for h in range(H_B):
            x = jnp.broadcast_to(frev_ref[h:h + 1, :], (LANES, ltot))
            skew[h] = pltpu.roll(x, ltot - LANES, 1, stride=1, stride_axis=0)

    @pl.when(i == 0)
    def _():
        hi, lo = _split_hi_lo(ki_ref[0])
        kcat[:, :LANES] = hi
        kcat[:, LANES:] = jnp.where(left, lo, jnp.zeros_like(lo))
        cke[:, :R_KV] = ckv_ref[0]
        one_col = lax.broadcasted_iota(jnp.int32, (L, LANES), 1) == 0
        cke[:, R_KV:] = jnp.where(one_col, 1.0, 0.0).astype(BF16)

    qcats, ws = [], []
    for h in range(H_IDX):
        hi, lo = _split_hi_lo(qi_ref[0, :, h * LANES:(h + 1) * LANES])
        qcats.append(jnp.concatenate([jnp.where(left, hi, lo), jnp.where(left, hi, jnp.zeros_like(hi))], axis=1))
        ws.append(gt_ref[0, :, 2 * H_A + h:2 * H_A + h + 1] * (H_IDX ** -0.5))

    def score_body(c, carry):
        s0 = pl.multiple_of(c * kc, kc)
        kblk = kcat[pl.ds(s0, kc), :]
        sc = jnp.zeros((tq, kc), F32)
        for h in range(H_IDX):
            sc = sc + ws[h] * jnp.maximum(lax.dot_general(qcats[h], kblk, (((1,), (1,)), ((), ())),
                                                          preferred_element_type=F32), 0.0)
        bits = lax.bitcast_convert_type(sc + 0.0, jnp.int32)
        key = jnp.where(bits < 0, bits ^ jnp.int32(0x7FFFFFFF), bits)
        skey[:, pl.ds(s0, kc)] = jnp.where(s0 + lane_k <= tpos, key, jnp.int32(INT_MIN))
        return carry

    lax.fori_loop(0, nck, score_body, 0)

    thr_ref[...] = jnp.full(thr_ref.shape, INT_MIN, jnp.int32)
    j_ref[...] = jnp.full(j_ref.shape, L, jnp.int32)
    rb = DSA_RB

    def rowblock_body(r, carry):
        r0 = pl.multiple_of(r * rb, rb)

        def count(pred):
            def body(c, acc):
                s0 = pl.multiple_of(c * kc, kc)
                hit = pred(skey[pl.ds(r0, rb), pl.ds(s0, kc)], s0 + lane_k).astype(jnp.int32)
                return acc + functools.reduce(lambda a, b: a + b,
                                              [hit[:, a * LANES:(a + 1) * LANES] for a in range(kc // LANES)])
            acc = lax.fori_loop(0, nck, body, jnp.zeros((rb, LANES), jnp.int32))
            return jnp.sum(acc, axis=1, keepdims=True)

        def bit_body(it, tu):
            cand_u = tu | (jnp.int32(1) << (31 - it))
            cand = cand_u ^ jnp.int32(INT_MIN)
            return jnp.where(count(lambda blk, idx: blk >= cand) >= topk, cand_u, tu)

        tu = lax.fori_loop(0, 32, bit_body, jnp.zeros((rb, 1), jnp.int32))
        thr = tu ^ jnp.int32(INT_MIN)
        thr_ref[pl.ds(r0, rb), :] = jnp.broadcast_to(thr, (rb, LANES))
        need = topk - count(lambda blk, idx: blk > thr)
        n_eq = count(lambda blk, idx: blk == thr)
        tied = ((n_eq > need) & (thr != INT_MIN)).astype(jnp.int32)

        @pl.when(jnp.max(tied) > 0)
        def _():
            nbits = max(1, (L - 1).bit_length())

            def jbit_body(it, jv):
                cand = jv | (jnp.int32(1) << (nbits - 1 - it))
                return jnp.where(count(lambda blk, idx: (blk == thr) & (idx < cand)) < need, cand, jv)

            jv = lax.fori_loop(0, nbits, jbit_body, jnp.zeros((rb, 1), jnp.int32))
            j_ref[pl.ds(r0, rb), :] = jnp.broadcast_to(jv, (rb, LANES))

        return carry

    @pl.when(t0 + tq > topk)
    def _():
        lax.fori_loop(0, tq // rb, rowblock_body, 0)

    thr = thr_ref[:, 0:1]
    jv = j_ref[:, 0:1]
    qabs = [(_bdot_nt(qb_ref[0, :, h * DH_B:(h + 1) * DH_B], wuk_ref[h]) * (DH_B ** -0.5)).astype(BF16)
            for h in range(H_B)]
    m_ref[...] = jnp.full(m_ref.shape, NEG_INF, F32)
    acc_ref[...] = jnp.zeros(acc_ref.shape, F32)

    def attn_body(c, carry):
        s0 = pl.multiple_of(c * kc, kc)
        blk = skey[:, pl.ds(s0, kc)]
        idx = s0 + lane_k
        sel = ((blk > thr) | ((blk == thr) & (idx <= jv))) & (idx <= tpos)
        madd = jnp.where(sel, 0.0, NEG_INF).astype(F32)
        ck = cke[pl.ds(s0, kc), :]
        for h in range(H_B):
            bias = jnp.concatenate(
                [skew[h, :, pl.ds(pl.multiple_of(L - (t0 - s0) - (a + 1) * LANES, LANES), kc)]
                 for a in range(tq // LANES)], axis=0)
            logits = _bdot_nt(qabs[h], ck[:, :R_KV]) + bias + madd
            m_old = m_ref[h][:, 0:1]
            m_new = jnp.maximum(m_old, jnp.max(logits, axis=1, keepdims=True))
            p = jnp.exp(logits - m_new)
            acc_ref[h] = jnp.exp(m_old - m_new) * acc_ref[h] + _bdot(p, ck)
            m_ref[h] = jnp.broadcast_to(m_new, m_ref.shape[1:])
        return carry

    lax.fori_loop(0, nck, attn_body, 0)
    for h in range(H_B):
        o_lat = acc_ref[h, :, :R_KV] / acc_ref[h, :, R_KV:R_KV + 1]
        o_ref[0, :, h * DH_B:(h + 1) * DH_B] = _bdot(o_lat, wuv_ref[h]).astype(o_ref.dtype)


def dsa_mixer(qb, qi, gates, ki, ckv, w_uk, w_uv, table, out_dtype=F32):
    bn, L, _ = qb.shape
    tq = min(DSA_TQ, L)
    topk = min(TOPK_MAX, L // TOPK_DIV)
    wuk = jnp.transpose(w_uk, (1, 0, 2)).astype(BF16)
    wuv = jnp.transpose(w_uv, (1, 0, 2)).astype(BF16)
    tt = table[:, :8].T
    frev = jnp.concatenate([jnp.zeros((8, 1), F32), tt[:, ::-1], jnp.zeros((8, DSA_KC - 1), F32)], axis=1)
    row = lambda b, i: (b, i, 0)
    full = lambda b, i: (b, 0, 0)
    c3 = lambda b, i: (0, 0, 0)
    kern = functools.partial(_dsa_kernel, topk=topk)
    return pl.pallas_call(
        kern,
        grid=(bn, L // tq),
        in_specs=[pl.BlockSpec((1, tq, H_B * DH_B), row),
                  pl.BlockSpec((1, tq, H_IDX * LANES), row),
                  pl.BlockSpec((1, tq, LANES), row),
                  pl.BlockSpec((1, L, LANES), full),
                  pl.BlockSpec((1, L, R_KV), full),
                  pl.BlockSpec(wuk.shape, c3),
                  pl.BlockSpec(wuv.shape, c3),
                  pl.BlockSpec(frev.shape, lambda b, i: (0, 0))],
        out_specs=pl.BlockSpec((1, tq, H_B * DH_B), row),
        out_shape=jax.ShapeDtypeStruct((bn, L, H_B * DH_B), out_dtype),
        scratch_shapes=[pltpu.VMEM((H_B, LANES, frev.shape[1]), F32),
                        pltpu.VMEM((L, 2 * LANES), BF16),
                        pltpu.VMEM((L, R_KV + LANES), BF16),
                        pltpu.VMEM((tq, L), jnp.int32),
                        pltpu.VMEM((tq, LANES), jnp.int32),
                        pltpu.VMEM((tq, LANES), jnp.int32),
                        pltpu.VMEM((H_B, tq, LANES), F32),
                        pltpu.VMEM((H_B, tq, R_KV + LANES), F32)],
        compiler_params=_cparams("arbitrary", "arbitrary"),
        name="dsa_mixer",
    )(qb, qi, gates, ki, ckv, wuk, wuv, frev)


def _router_gates(lg):
    lane = lax.broadcasted_iota(jnp.int32, lg.shape, 1)
    ninf = jnp.float32(-jnp.inf)
    lc = jnp.where(lane < N_GROUPS, lg, ninf)
    mc = jnp.max(lc, axis=1, keepdims=True)
    gidx = jnp.min(jnp.where(lc == mc, lane, LANES), axis=1, keepdims=True)
    p_g = 1.0 / jnp.sum(jnp.exp(lc - mc), axis=1, keepdims=True)
    lo = N_GROUPS + E_PER_GROUP * gidx
    lf = jnp.where((lane >= lo) & (lane < lo + E_PER_GROUP), lg, ninf)
    v1 = jnp.max(lf, axis=1, keepdims=True)
    i1 = jnp.min(jnp.where(lf == v1, lane, LANES), axis=1, keepdims=True)
    lf2 = jnp.where(lane == i1, ninf, lf)
    v2 = jnp.max(lf2, axis=1, keepdims=True)
    i2 = jnp.min(jnp.where(lf2 == v2, lane, LANES), axis=1, keepdims=True)
    e2 = jnp.exp(v2 - v1)
    p1 = 1.0 / (1.0 + e2)
    p2 = e2 / (1.0 + e2)
    return p_g * jnp.where(lane == i1, p1, jnp.where(lane == i2, p2, 0.0))


def _moe_dense_kernel(x_ref, g_ref, sc_ref, sh_ref, gt_ref, wr_ref, br_ref, wg_ref, wu_ref, wd_ref, gf_ref,
                      o_ref, hf, gates, acc, *, final):
    e = pl.program_id(2)

    @pl.when(e == 0)
    def _():
        h = _modulated_norm(x_ref[0], g_ref[...], sc_ref[0], sh_ref[0])
        hf[...] = h.astype(BF16)
        gates[...] = _router_gates(_fdot(h, wr_ref[...]) + br_ref[...])
        acc[...] = jnp.zeros(acc.shape, F32)

    lane = lax.broadcasted_iota(jnp.int32, gates.shape, 1)
    gcol = jnp.sum(jnp.where(lane == N_GROUPS + e, gates[...], 0.0), axis=1, keepdims=True)
    hb = hf[...]
    a = jnp.dot(hb, wg_ref[0], preferred_element_type=F32)
    a = a * jax.nn.sigmoid(a) * jnp.dot(hb, wu_ref[0], preferred_element_type=F32) * gcol
    acc[...] += _bdot(a, wd_ref[0])

    @pl.when(e == pl.num_programs(2) - 1)
    def _():
        y = x_ref[0] + gt_ref[0] * acc[...]
        if final:
            y = y * lax.rsqrt(jnp.mean(y * y, axis=-1, keepdims=True) + EPS) * gf_ref[...]
        o_ref[0] = y


def moe_layer(x, g, sc, sh, gt, w_coarse, b_coarse, w_fine, b_fine, w_gate, w_up, w_down, g_final=None, tm=1024):
    bn, L, d = x.shape
    ne = N_GROUPS * E_PER_GROUP
    ff = w_gate.shape[-1]
    wr = jnp.concatenate([w_coarse, jnp.transpose(w_fine, (1, 0, 2)).reshape(d, ne)], axis=1)
    wr = _pad_cols(wr, LANES)
    br = _pad_cols(jnp.concatenate([b_coarse, b_fine.reshape(-1)])[None, :], LANES)
    wg = w_gate.reshape(ne, d, ff).astype(BF16)
    wu = w_up.reshape(ne, d, ff).astype(BF16)
    wd = w_down.reshape(ne, ff, d).astype(BF16)
    final = g_final is not None
    gf = (g_final if final else jnp.ones((d,), F32)).reshape(1, d)
    row = lambda b, i, e: (b, i, 0)
    per_b = lambda b, i, e: (b, 0, 0)
    const = lambda b, i, e: (0, 0)
    per_e = lambda b, i, e: (e, 0, 0)
    kern = functools.partial(_moe_dense_kernel, final=final)
    return pl.pallas_call(
        kern,
        grid=(bn, L // tm, ne),
        in_specs=[pl.BlockSpec((1, tm, d), row),
                  pl.BlockSpec((1, d), const),
                  pl.BlockSpec((1, 1, d), per_b),
                  pl.BlockSpec((1, 1, d), per_b),
                  pl.BlockSpec((1, 1, d), per_b),
                  pl.BlockSpec((d, LANES), const),
                  pl.BlockSpec((1, LANES), const),
                  pl.BlockSpec((1, d, ff), per_e),
                  pl.BlockSpec((1, d, ff), per_e),
                  pl.BlockSpec((1, ff, d), per_e),
                  pl.BlockSpec((1, d), const)],
        out_specs=pl.BlockSpec((1, tm, d), row),
        out_shape=jax.ShapeDtypeStruct((bn, L, d), F32),
        scratch_shapes=[pltpu.VMEM((tm, d), BF16), pltpu.VMEM((tm, LANES), F32), pltpu.VMEM((tm, d), F32)],
        compiler_params=_cparams("parallel", "parallel", "arbitrary"),
        name="moe_layer",
    )(x, g.reshape(1, d), sc.reshape(bn, 1, d), sh.reshape(bn, 1, d), gt.reshape(bn, 1, d), wr, br, wg, wu, wd, gf)


def _pad_cols(w, width):
    return jnp.pad(w, ((0, 0), (0, width - w.shape[1])))


def _segments(widths):
    offs = np.cumsum((0,) + tuple(widths))
    return tuple((int(o), int(w)) for o, w in zip(offs[:-1], widths))


AB_WIDTHS = (2 * H_A * DK_A, H_A * DV_A, H_A * DV_A, LANES, H_B * DH_B, R_KV, H_IDX * LANES, LANES)
AB_DTYPES = (F32, F32, F32, F32, F32, BF16, F32, F32)
CD_WIDTHS = (2 * H_C * DK_C, H_C * DV_C, H_C * DV_C, LANES, H_D * DH_D, H_D * DH_D, H_D * DH_D)
CD_DTYPES = (F32,) * 7


def pack_ab_w_in(w):
    d = w.shape[0]
    nqk = 2 * H_A * DK_A
    nv = H_A * DV_A
    o_i = nqk + 2 * nv
    o_qb = o_i + 2 * H_A
    o_ckv = o_qb + H_B * DH_B
    o_qi = o_ckv + R_KV
    o_ki = o_qi + H_IDX * D_IDX
    o_wi = o_ki + D_IDX
    gates = _pad_cols(jnp.concatenate([w[:, o_i:o_qb], w[:, o_wi:o_wi + H_IDX]], axis=1), LANES)
    wqi = w[:, o_qi:o_ki].reshape(d, H_IDX, D_IDX)
    wqi = jnp.concatenate([wqi, wqi], axis=-1).reshape(d, H_IDX * LANES)
    wki = w[:, o_ki:o_wi]
    return jnp.concatenate([w[:, :o_i], gates, w[:, o_qb:o_ckv], w[:, o_ckv:o_qi], wqi, wki, wki],
                           axis=1).astype(BF16)


def pack_cd_w_in(w):
    o_g = 2 * H_C * DK_C + H_C * DV_C
    o_r = o_g + GLA_RANK
    o_q = o_r + H_C * DV_C
    return jnp.concatenate([w[:, :o_g], w[:, o_r:o_q], _pad_cols(w[:, o_g:o_r], LANES), w[:, o_q:]],
                           axis=1).astype(BF16)


def kernel(x, c, w_ada, b_ada, g_mix, g_ffn, g_final, rel_bias, ab_w_in, ab_conv_w, ab_conv_b, ab_gate_b, ab_hnorm_g, ab_w_uk, ab_w_uv, ab_w_out, cd_w_in, cd_w_alpha, cd_b_alpha, cd_hnorm_g, cd_w_out, moe_w_coarse, moe_b_coarse, moe_w_fine, moe_b_fine, moe_w_gate, moe_w_up, moe_w_down):
    depth = w_ada.shape[0]
    L, d = x.shape[1], x.shape[2]
    mod = ada_mod(c, w_ada, b_ada)
    table = bias_table(rel_bias, L)
    for l in range(depth):
        sh_m, sc_m, gt_m, sh_f, sc_f, gt_f = [mod[l, :, i * d:(i + 1) * d] for i in range(6)]
        j = l // 2
        if l % 2 == 0:
            qk, va, oa, gates, qb, ckv, qi, ki = norm_proj(
                x, g_mix[l], sc_m, sh_m, pack_ab_w_in(ab_w_in[j]), _segments(AB_WIDTHS), AB_DTYPES)
            y1 = mlstm_mixer(qk, va, oa, gates, ab_conv_w[j], ab_conv_b[j], ab_gate_b[j], ab_hnorm_g[j])
            y2 = dsa_mixer(qb, qi, gates, ki, ckv, ab_w_uk[j], ab_w_uv[j], table)
            w_out = ab_w_out[j]
        else:
            qk, vc, rc, gc, qd, kd, vd = norm_proj(
                x, g_mix[l], sc_m, sh_m, pack_cd_w_in(cd_w_in[j]), _segments(CD_WIDTHS), CD_DTYPES)
            y1 = gla_mixer(qk, vc, gc, rc, cd_w_alpha[j], cd_b_alpha[j], cd_hnorm_g[j])
            y2 = dilated_mixer(qd, kd, vd, rel_bias)
            w_out = cd_w_out[j]
        x = outproj_residual(x, y1, y2, gt_m, w_out.astype(BF16))
        x = moe_layer(x, g_ffn[l], sc_f, sh_f, gt_f, moe_w_coarse[l], moe_b_coarse[l], moe_w_fine[l],
                      moe_b_fine[l], moe_w_gate[l], moe_w_up[l], moe_w_down[l],
                      g_final=g_final if l == depth - 1 else None)
    return x
```

```python
import functools
import math

import numpy as np
import jax
import jax.numpy as jnp
from jax import lax
from jax.experimental import pallas as pl
from jax.experimental.pallas import tpu as pltpu

F32 = jnp.float32
BF16 = jnp.bfloat16

EPS = 1e-6
NEG_INF = -1e30

H_A, DK_A, DV_A, CONV_A, MLSTM_CHUNK = 4, 64, 128, 4, 64
H_B, DH_B, R_KV, H_IDX, D_IDX = 8, 64, 128, 4, 64
TOPK_MAX, TOPK_DIV = 256, 4
H_C, DK_C, DV_C, GLA_RANK, GLA_TAU, GLA_CHUNK = 4, 64, 128, 16, 16.0, 64
H_D, DH_D = 8, 64
DIL_PATTERNS = ((128, 1), (512, 4), (2048, 16))
N_BUCKETS, MAX_DIST = 32, 2048
N_GROUPS, E_PER_GROUP, EXPERT_FF = 4, 4, 512

LANES = 128
VMEM_LIMIT = 56 * 1024 * 1024


def _cparams(*sem):
    return pltpu.CompilerParams(dimension_semantics=sem, vmem_limit_bytes=VMEM_LIMIT)


def _bdot(a, b):
    return jnp.dot(a.astype(BF16), b.astype(BF16), preferred_element_type=F32)


def _bdot_nt(a, b):
    return lax.dot_general(a.astype(BF16), b.astype(BF16), (((1,), (1,)), ((), ())),
                           preferred_element_type=F32)


def _fdot(a, b):
    return jnp.dot(a, b, preferred_element_type=F32, precision=lax.Precision.HIGHEST)


def _ada_kernel(c_ref, w_ref, b_ref, o_ref):
    c = c_ref[...]
    a = c * jax.nn.sigmoid(c)
    o_ref[0] = _fdot(a, w_ref[0]) + b_ref[0]


def ada_mod(c, w_ada, b_ada):
    depth, d, e = w_ada.shape
    bn = c.shape[0]
    tn = 1536
    return pl.pallas_call(
        _ada_kernel,
        grid=(depth, e // tn),
        in_specs=[pl.BlockSpec((bn, d), lambda l, j: (0, 0)),
                  pl.BlockSpec((1, d, tn), lambda l, j: (l, 0, j)),
                  pl.BlockSpec((1, 1, tn), lambda l, j: (l, 0, j))],
        out_specs=pl.BlockSpec((1, bn, tn), lambda l, j: (l, 0, j)),
        out_shape=jax.ShapeDtypeStruct((depth, bn, e), F32),
        compiler_params=_cparams("arbitrary", "arbitrary"),
        name="ada_mod",
    )(c, w_ada, b_ada.reshape(depth, 1, e))


def _modulated_norm(x, g, sc, sh):
    y = x * lax.rsqrt(jnp.mean(x * x, axis=-1, keepdims=True) + EPS) * g
    return y * (1.0 + sc) + sh


def _norm_proj_kernel(x_ref, g_ref, sc_ref, sh_ref, w_ref, *o_refs, segs):
    h = _modulated_norm(x_ref[0], g_ref[...], sc_ref[0], sh_ref[0]).astype(BF16)
    for o_ref, (start, width) in zip(o_refs, segs):
        o_ref[0] = jnp.dot(h, w_ref[:, start:start + width], preferred_element_type=F32).astype(o_ref.dtype)


def norm_proj(x, g, sc, sh, w, segs, dtypes, tm=512):
    bn, L, d = x.shape
    wp = w.shape[1]
    kern = functools.partial(_norm_proj_kernel, segs=tuple(segs))
    return pl.pallas_call(
        kern,
        grid=(bn, L // tm),
        in_specs=[pl.BlockSpec((1, tm, d), lambda b, i: (b, i, 0)),
                  pl.BlockSpec((1, d), lambda b, i: (0, 0)),
                  pl.BlockSpec((1, 1, d), lambda b, i: (b, 0, 0)),
                  pl.BlockSpec((1, 1, d), lambda b, i: (b, 0, 0)),
                  pl.BlockSpec((d, wp), lambda b, i: (0, 0))],
        out_specs=[pl.BlockSpec((1, tm, wd), lambda b, i: (b, i, 0)) for (_, wd) in segs],
        out_shape=[jax.ShapeDtypeStruct((bn, L, wd), dt) for (_, wd), dt in zip(segs, dtypes)],
        compiler_params=_cparams("parallel", "parallel"),
        name="norm_proj",
    )(x, g.reshape(1, d), sc.reshape(bn, 1, d), sh.reshape(bn, 1, d), w)


def _outproj_kernel(x_ref, ya_ref, yb_ref, gt_ref, w_ref, o_ref):
    na = ya_ref.shape[-1]
    y = _bdot(ya_ref[0], w_ref[:na, :]) + _bdot(yb_ref[0], w_ref[na:, :])
    o_ref[0] = x_ref[0] + gt_ref[0] * y


def outproj_residual(x, ya, yb, gt, w, tm=512):
    bn, L, d = x.shape
    na, nb = ya.shape[-1], yb.shape[-1]
    return pl.pallas_call(
        _outproj_kernel,
        grid=(bn, L // tm),
        in_specs=[pl.BlockSpec((1, tm, d), lambda b, i: (b, i, 0)),
                  pl.BlockSpec((1, tm, na), lambda b, i: (b, i, 0)),
                  pl.BlockSpec((1, tm, nb), lambda b, i: (b, i, 0)),
                  pl.BlockSpec((1, 1, d), lambda b, i: (b, 0, 0)),
                  pl.BlockSpec((na + nb, d), lambda b, i: (0, 0))],
        out_specs=pl.BlockSpec((1, tm, d), lambda b, i: (b, i, 0)),
        out_shape=jax.ShapeDtypeStruct((bn, L, d), F32),
        compiler_params=_cparams("parallel", "parallel"),
        name="outproj_residual",
    )(x, ya, yb, gt.reshape(bn, 1, d), w)


def _log_sigmoid(x):
    return jnp.minimum(x, 0.0) - jnp.log1p(jnp.exp(-jnp.abs(x)))


def _dot_tn(a, b):
    return lax.dot_general(a.astype(BF16), b.astype(BF16), (((0,), (0,)), ((), ())),
                           preferred_element_type=F32)


def _tri(T):
    r = np.arange(T)
    return (r[:, None] >= r[None, :]).astype(np.float32)


def _mlstm_kernel(qk_ref, v_ref, og_ref, gt_ref, cw_ref, cb_ref, gb_ref, hg_ref, ltri_ref, utri_ref,
                  o_ref, xbuf, cext, mstate):
    T = qk_ref.shape[1]
    tail = xbuf.shape[0] - T

    @pl.when(pl.program_id(1) == 0)
    def _():
        xbuf[0:tail, :] = jnp.zeros((tail, xbuf.shape[1]), F32)
        cext[...] = jnp.zeros(cext.shape, F32)
        mstate[...] = jnp.full(mstate.shape, NEG_INF, F32)

    x = qk_ref[0]
    xbuf[tail:tail + T, :] = x
    conv = cb_ref[...] + sum(xbuf[pl.ds(tail - (CONV_A - 1) + j, T), :] * cw_ref[j:j + 1, :]
                             for j in range(CONV_A))
    xbuf[0:tail, :] = x[T - tail:T, :]
    qk = conv * jax.nn.sigmoid(conv)

    G = gt_ref[0] + gb_ref[...]
    GT = G.T
    bcol = _fdot(ltri_ref[...], _log_sigmoid(G))
    brow = _fdot(_log_sigmoid(GT), utri_ref[...])
    causal = ltri_ref[...] > 0.5
    lane = lax.broadcasted_iota(jnp.int32, (T, DV_A), 1)
    ones_blk = jnp.where(lane == 0, 1.0, 0.0).astype(F32)
    nq = H_A * DK_A

    for h in range(H_A):
        b_c = bcol[:, H_A + h:H_A + h + 1]
        b_r = brow[H_A + h:H_A + h + 1, :]
        li_r = GT[h:h + 1, :]
        li_c = G[:, h:h + 1]
        m_prev = mstate[h:h + 1, 0:1]
        Dm = jnp.where(causal, b_c - b_r + li_r, NEG_INF)
        g = b_c + m_prev
        mt = jnp.maximum(g, jnp.max(Dm, axis=1, keepdims=True))
        w_intra = jnp.exp(Dm - mt)
        w_state = jnp.exp(g - mt)
        q = qk[:, h * DK_A:(h + 1) * DK_A] * (DK_A ** -0.5)
        k = qk[:, nq + h * DK_A:nq + (h + 1) * DK_A]
        vext = jnp.concatenate([v_ref[0, :, h * DV_A:(h + 1) * DV_A].astype(F32), ones_blk], axis=1)
        s = _bdot_nt(q, k) * w_intra
        nd = w_state * _bdot(q, cext[h]) + _bdot(s, vext)
        num = nd[:, :DV_A]
        den = nd[:, DV_A:DV_A + 1]
        hout = num / jnp.maximum(jnp.abs(den), jnp.exp(-mt))
        bL = b_c[T - 1:T, :]
        d_end = bL - b_c + li_c
        m_new = jnp.maximum(bL + m_prev, jnp.max(d_end, axis=0, keepdims=True))
        w_e = jnp.exp(d_end - m_new)
        decay = jnp.exp(bL + m_prev - m_new)
        cext[h] = decay * cext[h] + _dot_tn(k * w_e, vext)
        mstate[h:h + 1, :] = jnp.broadcast_to(m_new, (1, mstate.shape[1]))
        y = hout * lax.rsqrt(jnp.mean(hout * hout, axis=-1, keepdims=True) + EPS)
        y = y * hg_ref[:, h * DV_A:(h + 1) * DV_A]
        o_ref[0, :, h * DV_A:(h + 1) * DV_A] = (
            y * jax.nn.sigmoid(og_ref[0, :, h * DV_A:(h + 1) * DV_A])).astype(o_ref.dtype)


def mlstm_mixer(qk, v, og, gates, conv_w, conv_b, gate_b, hnorm_g, out_dtype=F32):
    bn, L, _ = qk.shape
    T = MLSTM_CHUNK
    nv = H_A * DV_A
    gb = jnp.zeros((1, LANES), F32).at[0, :2 * H_A].set(gate_b.reshape(-1))
    ltri = jnp.asarray(_tri(T))
    row = lambda b, i: (b, i, 0)
    const = lambda b, i: (0, 0)
    return pl.pallas_call(
        _mlstm_kernel,
        grid=(bn, L // T),
        in_specs=[pl.BlockSpec((1, T, qk.shape[-1]), row),
                  pl.BlockSpec((1, T, nv), row),
                  pl.BlockSpec((1, T, nv), row),
                  pl.BlockSpec((1, T, LANES), row),
                  pl.BlockSpec(conv_w.shape, const),
                  pl.BlockSpec((1, conv_b.shape[-1]), const),
                  pl.BlockSpec((1, LANES), const),
                  pl.BlockSpec((1, nv), const),
                  pl.BlockSpec((T, T), const),
                  pl.BlockSpec((T, T), const)],
        out_specs=pl.BlockSpec((1, T, nv), row),
        out_shape=jax.ShapeDtypeStruct((bn, L, nv), out_dtype),
        scratch_shapes=[pltpu.VMEM((8 + T, qk.shape[-1]), F32),
                        pltpu.VMEM((H_A, DK_A, 2 * DV_A), F32),
                        pltpu.VMEM((8, LANES), F32)],
        compiler_params=_cparams("parallel", "arbitrary"),
        name="mlstm_mixer",
    )(qk, v, og, gates, conv_w, conv_b.reshape(1, -1), gb, hnorm_g.reshape(1, -1), ltri, ltri.T)


def _gla_level_consts(T):
    r = np.arange(T)
    mq, mk, bm = [], [], []
    n = T
    while n >= 2:
        start = (r // n) * n
        mid = start + n // 2
        upper = r >= mid
        mq.append((upper[:, None] & (r[None, :] >= mid[:, None]) & (r[None, :] <= r[:, None])).astype(np.float32))
        mk.append((~upper[:, None] & (r[None, :] > r[:, None]) & (r[None, :] < mid[:, None])).astype(np.float32))
        bm.append((upper[:, None] & ~upper[None, :] & (start[:, None] == start[None, :])).astype(np.float32))
        n //= 2
    bm.append(np.eye(T, dtype=np.float32))
    return np.concatenate(mq, 0), np.concatenate(mk, 0), np.stack(bm, 0)


def _gla_kernel(qk_ref, v_ref, gc_ref, rc_ref, wa_ref, ba_ref, hg_ref, ltri_ref, mq_ref, mk_ref, bm_ref,
                o_ref, st):
    T = qk_ref.shape[1]
    nlev = bm_ref.shape[0] - 1
    nq = H_C * DK_C

    @pl.when(pl.program_id(1) == 0)
    def _():
        st[...] = jnp.zeros(st.shape, F32)

    qk = qk_ref[0]
    q = qk[:, :nq] * (DK_C ** -0.5)
    k = qk[:, nq:]
    la = _log_sigmoid(_bdot(gc_ref[0], wa_ref[...]) + ba_ref[...]) * (1.0 / GLA_TAU)
    eq = jnp.exp(_fdot(mq_ref[...], la))
    ek = jnp.exp(_fdot(mk_ref[...], la))
    bfull = _fdot(ltri_ref[...], la)
    bL = bfull[T - 1:T, :]
    qe = q * jnp.exp(bfull)
    kd = k * jnp.exp(bL - bfull)
    eL = jnp.exp(bL)

    for h in range(H_C):
        cs = slice(h * DK_C, (h + 1) * DK_C)
        vs = slice(h * DV_C, (h + 1) * DV_C)
        qh, kh, vh = q[:, cs], k[:, cs], v_ref[0, :, vs]
        A = bm_ref[nlev] * _bdot_nt(qh, kh)
        for lv in range(nlev):
            A = A + bm_ref[lv] * _bdot_nt(qh * eq[lv * T:(lv + 1) * T, cs], kh * ek[lv * T:(lv + 1) * T, cs])
        o = _bdot_nt(qe[:, cs], st[h]) + _bdot(A, vh)
        st[h] = eL[:, cs] * st[h] + _dot_tn(vh, kd[:, cs])
        y = o * lax.rsqrt(jnp.mean(o * o, axis=-1, keepdims=True) + EPS) * hg_ref[:, vs]
        r = rc_ref[0, :, vs]
        o_ref[0, :, vs] = (y * (r * jax.nn.sigmoid(r))).astype(o_ref.dtype)


def gla_mixer(qk, v, gc, rc, w_alpha, b_alpha, hnorm_g, out_dtype=F32):
    bn, L, _ = qk.shape
    T = GLA_CHUNK
    nv = H_C * DV_C
    nq = H_C * DK_C
    wa = jnp.pad(w_alpha, ((0, gc.shape[-1] - w_alpha.shape[0]), (0, 0))).astype(BF16)
    mq, mk, bm = _gla_level_consts(T)
    row = lambda b, i: (b, i, 0)
    const = lambda b, i: (0, 0)
    return pl.pallas_call(
        _gla_kernel,
        grid=(bn, L // T),
        in_specs=[pl.BlockSpec((1, T, 2 * nq), row),
                  pl.BlockSpec((1, T, nv), row),
                  pl.BlockSpec((1, T, gc.shape[-1]), row),
                  pl.BlockSpec((1, T, nv), row),
                  pl.BlockSpec(wa.shape, const),
                  pl.BlockSpec((1, nq), const),
                  pl.BlockSpec((1, nv), const),
                  pl.BlockSpec((T, T), const),
                  pl.BlockSpec(mq.shape, const),
                  pl.BlockSpec(mk.shape, const),
                  pl.BlockSpec(bm.shape, lambda b, i: (0, 0, 0))],
        out_specs=pl.BlockSpec((1, T, nv), row),
        out_shape=jax.ShapeDtypeStruct((bn, L, nv), out_dtype),
        scratch_shapes=[pltpu.VMEM((H_C, DV_C, DK_C), F32)],
        compiler_params=_cparams("parallel", "arbitrary"),
        name="gla_mixer",
    )(qk, v, gc, rc, wa, b_alpha.reshape(1, nq), hnorm_g.reshape(1, nv), jnp.asarray(_tri(T)),
      jnp.asarray(mq), jnp.asarray(mk), jnp.asarray(bm))


def _rel_bucket(dist):
    n = jnp.maximum(dist, 0)
    max_exact = N_BUCKETS // 2
    nf = jnp.maximum(n, 1).astype(F32)
    large = max_exact + (jnp.log(nf / max_exact) / math.log(MAX_DIST / max_exact)
                         * (N_BUCKETS - max_exact)).astype(jnp.int32)
    large = jnp.minimum(large, N_BUCKETS - 1)
    return jnp.where(n < max_exact, n, large)


def _bias_lookup_kernel(bk_ref, rb_ref, o_ref):
    n = bk_ref.shape[0]
    onehot = (lax.broadcasted_iota(jnp.int32, (n, LANES), 1) == bk_ref[...]).astype(F32)
    o_ref[...] = _fdot(onehot, rb_ref[...])


def bias_lookup(rel_bias, dist):
    n = dist.shape[0]
    tn = min(n, 4096)
    bk = jnp.broadcast_to(_rel_bucket(dist)[:, None], (n, LANES))
    rb = jnp.pad(rel_bias, ((0, LANES - rel_bias.shape[0]), (0, LANES - rel_bias.shape[1])))
    return pl.pallas_call(
        _bias_lookup_kernel,
        grid=(n // tn,),
        in_specs=[pl.BlockSpec((tn, LANES), lambda i: (i, 0)), pl.BlockSpec((LANES, LANES), lambda i: (0, 0))],
        out_specs=pl.BlockSpec((tn, LANES), lambda i: (i, 0)),
        out_shape=jax.ShapeDtypeStruct((n, LANES), F32),
        compiler_params=_cparams("parallel"),
        name="bias_lookup",
    )(bk, rb)


def bias_table(rel_bias, L):
    return bias_lookup(rel_bias, jnp.arange(L, dtype=jnp.int32))


def _dilated_kernel(q_ref, kp_ref, kc_ref, vp_ref, vc_ref, bias_ref, num_ref, m_ref, den_ref):
    W = q_ref.shape[1]
    first = pl.program_id(1) == 0
    col = lax.broadcasted_iota(jnp.int32, (1, 2 * W), 1)
    edge = jnp.where((col < W) & first, NEG_INF, 0.0).astype(F32)
    for h in range(H_D):
        cs = slice(h * DH_D, (h + 1) * DH_D)
        qh = q_ref[0, :, cs] * (DH_D ** -0.5)
        kcat = jnp.concatenate([kp_ref[0, :, cs], kc_ref[0, :, cs]], axis=0)
        vcat = jnp.concatenate([vp_ref[0, :, cs], vc_ref[0, :, cs]], axis=0)
        logits = _bdot_nt(qh, kcat) + bias_ref[h] + edge
        m = jnp.max(logits, axis=1, keepdims=True)
        p = jnp.exp(logits - m)
        num_ref[0, :, cs] = _bdot(p, vcat)
        m_ref[0, :, cs] = jnp.broadcast_to(m, (W, DH_D))
        den_ref[0, :, cs] = jnp.broadcast_to(jnp.sum(p, axis=1, keepdims=True), (W, DH_D))


def dilated_partial(q, k, v, rel_bias, window, dil):
    bn, L, nd = q.shape
    W = window // dil
    M = L // dil

    def to_sub(a):
        return a.reshape(bn, M, dil, nd).transpose(0, 2, 1, 3).reshape(bn * dil, M, nd)

    def from_sub(a):
        return a.reshape(bn, dil, M, nd).transpose(0, 2, 1, 3).reshape(bn, L, nd)

    tap = np.arange(W)[:, None] + W - np.arange(2 * W)[None, :]
    ok = (tap >= 0) & (tap <= W)
    tile = bias_lookup(rel_bias, jnp.asarray((np.clip(tap, 0, W) * dil).reshape(-1), jnp.int32))
    bias = jnp.where(jnp.asarray(ok)[None], tile[:, :H_D].reshape(W, 2 * W, H_D).transpose(2, 0, 1), NEG_INF)
    qs, ks, vs = to_sub(q), to_sub(k), to_sub(v)
    cur = lambda i, j: (i, j, 0)
    prev = lambda i, j: (i, jnp.maximum(j - 1, 0), 0)
    blk = (1, W, nd)
    outs = pl.pallas_call(
        _dilated_kernel,
        grid=(bn * dil, M // W),
        in_specs=[pl.BlockSpec(blk, cur), pl.BlockSpec(blk, prev), pl.BlockSpec(blk, cur),
                  pl.BlockSpec(blk, prev), pl.BlockSpec(blk, cur),
                  pl.BlockSpec((H_D, W, 2 * W), lambda i, j: (0, 0, 0))],
        out_specs=[pl.BlockSpec(blk, cur)] * 3,
        out_shape=[jax.ShapeDtypeStruct(qs.shape, F32)] * 3,
        compiler_params=_cparams("parallel", "parallel"),
        name="dilated_partial",
    )(qs, ks, ks, vs, vs, bias)
    return [from_sub(a) for a in outs]


def _dilated_combine_kernel(*refs):
    o_ref = refs[-1]
    parts = [(refs[3 * i][0], refs[3 * i + 1][0], refs[3 * i + 2][0]) for i in range(len(DIL_PATTERNS))]
    m_all = functools.reduce(jnp.maximum, [p[1] for p in parts])
    ws = [jnp.exp(p[1] - m_all) for p in parts]
    num = sum(p[0] * w for p, w in zip(parts, ws))
    den = sum(p[2] * w for p, w in zip(parts, ws))
    o_ref[0] = (num / den).astype(o_ref.dtype)


def dilated_mixer(q, k, v, rel_bias, out_dtype=F32, tm=512):
    bn, L, nd = q.shape
    parts = []
    for window, dil in DIL_PATTERNS:
        parts += dilated_partial(q, k, v, rel_bias, window, dil)
    row = lambda b, i: (b, i, 0)
    return pl.pallas_call(
        _dilated_combine_kernel,
        grid=(bn, L // tm),
        in_specs=[pl.BlockSpec((1, tm, nd), row)] * len(parts),
        out_specs=pl.BlockSpec((1, tm, nd), row),
        out_shape=jax.ShapeDtypeStruct((bn, L, nd), out_dtype),
        compiler_params=_cparams("parallel", "parallel"),
        name="dilated_combine",
    )(*parts)


INT_MIN = -2 ** 31
DSA_TQ = 256
DSA_KC = 512
DSA_AR = 64


def _split_hi_lo(x):
    hi = x.astype(BF16)
    lo = (x - hi.astype(F32)).astype(BF16)
    return hi, lo


def _dsa_kernel(qb_ref, qi_ref, gt_ref, ki_ref, ckv_ref, wuk_ref, wuv_ref, ftab_ref, o_ref,
                skew, kcat, cke, cket, skey, thr_ref, j_ref, m_ref, acc_ref, madd_ref, lg0_ref, lg1_ref,
                p0_ref, p1_ref, qabs_ref, qcat_ref, alpha_ref, *, topk):
    tq = qb_ref.shape[1]
    L = ki_ref.shape[1]
    kc = DSA_KC
    ar = DSA_AR
    i = pl.program_id(1)
    t0 = i * tq
    nck = (t0 + tq + kc - 1) // kc
    left = lax.broadcasted_iota(jnp.int32, (1, LANES), 1) < (LANES // 2)
    tpos = t0 + lax.broadcasted_iota(jnp.int32, (1, tq), 1)
    krow = lax.broadcasted_iota(jnp.int32, (kc, tq), 0)

    @pl.when((pl.program_id(0) == 0) & (i == 0))
    def _():
        ltot = ftab_ref.shape[1]
        for h in range(H_B):
            x = jnp.broadcast_to(ftab_ref[h:h + 1, :], (LANES, ltot))
            skew[h] = pltpu.roll(x, ltot - LANES, 1, stride=1, stride_axis=0)

    @pl.when(i == 0)
    def _():
        hi, lo = _split_hi_lo(ki_ref[0])
        kcat[:, :LANES] = hi
        kcat[:, LANES:] = jnp.where(left, lo, jnp.zeros_like(lo))
        cke[:, :R_KV] = ckv_ref[0]
        one_col = lax.broadcasted_iota(jnp.int32, (L, LANES), 1) == 0
        cke[:, R_KV:] = jnp.where(one_col, 1.0, 0.0).astype(BF16)
        nr = cket.shape[0]
        eye = (lax.broadcasted_iota(jnp.int32, (nr, R_KV + LANES), 0)
               == lax.broadcasted_iota(jnp.int32, (nr, R_KV + LANES), 1)).astype(BF16)
        for c in range(L // kc):
            cket[:, c * kc:(c + 1) * kc] = _bdot_nt(eye, cke[c * kc:(c + 1) * kc, :]).astype(BF16)

    gt_t = gt_ref[0].T
    ws = []
    for h in range(H_IDX):
        hi, lo = _split_hi_lo(qi_ref[0, :, h * LANES:(h + 1) * LANES])
        qcat_ref[h, :, :LANES] = jnp.where(left, hi, lo)
        qcat_ref[h, :, LANES:] = jnp.where(left, hi, jnp.zeros_like(hi))
        ws.append(gt_t[2 * H_A + h:2 * H_A + h + 1, :] * (H_IDX ** -0.5))

    def score_body(c, carry):
        s0 = pl.multiple_of(c * kc, kc)
        sc = jnp.zeros((kc, tq), F32)
        for h in range(H_IDX):
            sc = sc + ws[h] * jnp.maximum(lax.dot_general(kcat[pl.ds(s0, kc), :], qcat_ref[h],
                                                          (((1,), (1,)), ((), ())),
                                                          preferred_element_type=F32), 0.0)
        bits = lax.bitcast_convert_type(sc + 0.0, jnp.int32)
        key = jnp.where(bits < 0, bits ^ jnp.int32(0x7FFFFFFF), bits)
        skey[pl.ds(s0, kc), :] = jnp.where(s0 + krow <= tpos, key, jnp.int32(INT_MIN))
        return carry

    lax.fori_loop(0, nck, score_body, 0)

    arow = lax.broadcasted_iota(jnp.int32, (ar, tq), 0)

    def count(pred):
        def body(c, acc):
            for a in range(kc // ar):
                r0 = pl.multiple_of(c * kc + a * ar, ar)
                acc = acc + pred(skey[pl.ds(r0, ar), :], r0 + arow).astype(jnp.int32)
            return acc
        acc = lax.fori_loop(0, nck, body, jnp.zeros((ar, tq), jnp.int32))
        return jnp.sum(acc, axis=0, keepdims=True)

    thr_ref[...] = jnp.full(thr_ref.shape, INT_MIN, jnp.int32)
    j_ref[...] = jnp.full(j_ref.shape, L, jnp.int32)

    @pl.when(t0 + tq > topk)
    def _():
        def bit_body(it, tu):
            cand_u = tu | (jnp.int32(1) << (31 - it))
            cand = cand_u ^ jnp.int32(INT_MIN)
            return jnp.where(count(lambda blk, idx: blk >= cand) >= topk, cand_u, tu)

        tu = lax.fori_loop(0, 32, bit_body, jnp.zeros((1, tq), jnp.int32))
        thr = tu ^ jnp.int32(INT_MIN)
        thr_ref[...] = jnp.broadcast_to(thr, thr_ref.shape)
        need = topk - count(lambda blk, idx: blk > thr)
        n_eq = count(lambda blk, idx: blk == thr)
        tied = ((n_eq > need) & (thr != INT_MIN)).astype(jnp.int32)

        @pl.when(jnp.max(tied) > 0)
        def _():
            nbits = max(1, (L - 1).bit_length())

            def jbit_body(it, jv):
                cand = jv | (jnp.int32(1) << (nbits - 1 - it))
                return jnp.where(count(lambda blk, idx: (blk == thr) & (idx < cand)) < need, cand, jv)

            jv = lax.fori_loop(0, nbits, jbit_body, jnp.zeros((1, tq), jnp.int32))
            j_ref[...] = jnp.broadcast_to(jv, j_ref.shape)

    thr = thr_ref[0:1, :]
    jv = j_ref[0:1, :]
    for h in range(H_B):
        qabs_ref[h] = (_bdot_nt(qb_ref[0, :, h * DH_B:(h + 1) * DH_B], wuk_ref[h]) * (DH_B ** -0.5)).astype(BF16)
    m_ref[...] = jnp.full(m_ref.shape, NEG_INF, F32)
    acc_ref[...] = jnp.zeros(acc_ref.shape, F32)

    sb = LANES
    nsb = kc // sb
    srow = lax.broadcasted_iota(jnp.int32, (sb, tq), 0)

    def attn_body(c, carry):
        s0 = pl.multiple_of(c * kc, kc)
        for a in range(nsb):
            blk = skey[pl.ds(s0 + a * sb, sb), :]
            idx = s0 + a * sb + srow
            sel = ((blk > thr) | ((blk == thr) & (idx <= jv))) & (idx <= tpos)
            madd_ref[a * sb:(a + 1) * sb, :] = jnp.where(sel, 0.0, NEG_INF).astype(F32)
        def scores(h, lg):
            lg[...] = _bdot_nt(cke[pl.ds(s0, kc), :R_KV], qabs_ref[h])

        def softmax(h, lg, pb):
            m_old = m_ref[h, 0:1, :]
            m_new = m_old
            for a in range(nsb):
                rows = slice(a * sb, (a + 1) * sb)
                off = pl.multiple_of(t0 - s0 + kc - (a + 1) * sb, sb)
                x = lg[rows, :] + skew[h, :, pl.ds(off, tq)] + madd_ref[rows, :]
                lg[rows, :] = x
                m_new = jnp.maximum(m_new, jnp.max(x, axis=0, keepdims=True))
            for a in range(nsb):
                rows = slice(a * sb, (a + 1) * sb)
                pb[rows, :] = jnp.exp(lg[rows, :] - m_new).astype(BF16)
            alpha_ref[h] = jnp.broadcast_to(jnp.exp(m_old - m_new), alpha_ref.shape[1:])
            m_ref[h] = jnp.broadcast_to(m_new, m_ref.shape[1:])

        def accumulate(h, pb):
            acc_ref[h] = alpha_ref[h, 0:1, :] * acc_ref[h] + jnp.dot(cket[:, pl.ds(s0, kc)], pb[...],
                                                                    preferred_element_type=F32)

        scores(0, lg0_ref)
        scores(1, lg1_ref)
        softmax(0, lg0_ref, p0_ref)
        scores(2, lg0_ref)
        accumulate(0, p0_ref)
        softmax(1, lg1_ref, p1_ref)

        def pair_body(k, carry2):
            e = 2 * k
            scores(e + 1, lg1_ref)
            accumulate(e - 1, p1_ref)
            softmax(e, lg0_ref, p0_ref)
            scores((e + 2) % H_B, lg0_ref)
            accumulate(e, p0_ref)
            softmax(e + 1, lg1_ref, p1_ref)
            return carry2

        lax.fori_loop(1, H_B // 2, pair_body, 0)
        accumulate(H_B - 1, p1_ref)
        return carry

    lax.fori_loop(0, nck, attn_body, 0)
    for h in range(H_B):
        o_lat = acc_ref[h, :R_KV, :] / acc_ref[h, R_KV:R_KV + 1, :]
        o_ref[0, :, h * DH_B:(h + 1) * DH_B] = _dot_tn(o_lat, wuv_ref[h]).astype(o_ref.dtype)


def dsa_mixer(qb, qi, gates, ki, ckv, w_uk, w_uv, table, out_dtype=F32):
    bn, L, _ = qb.shape
    tq = min(DSA_TQ, L)
    topk = min(TOPK_MAX, L // TOPK_DIV)
    wuk = jnp.transpose(w_uk, (1, 0, 2)).astype(BF16)
    wuv = jnp.transpose(w_uv, (1, 0, 2)).astype(BF16)
    ftab = jnp.concatenate([jnp.zeros((8, DSA_KC), F32), table[:, :8].T], axis=1)
    row = lambda b, i: (b, i, 0)
    full = lambda b, i: (b, 0, 0)
    c3 = lambda b, i: (0, 0, 0)
    kern = functools.partial(_dsa_kernel, topk=topk)
    return pl.pallas_call(
        kern,
        grid=(bn, L // tq),
        in_specs=[pl.BlockSpec((1, tq, H_B * DH_B), row),
                  pl.BlockSpec((1, tq, H_IDX * LANES), row),
                  pl.BlockSpec((1, tq, LANES), row),
                  pl.BlockSpec((1, L, LANES), full),
                  pl.BlockSpec((1, L, R_KV), full),
                  pl.BlockSpec(wuk.shape, c3),
                  pl.BlockSpec(wuv.shape, c3),
                  pl.BlockSpec(ftab.shape, lambda b, i: (0, 0))],
        out_specs=pl.BlockSpec((1, tq, H_B * DH_B), row),
        out_shape=jax.ShapeDtypeStruct((bn, L, H_B * DH_B), out_dtype),
        scratch_shapes=[pltpu.VMEM((H_B, LANES, ftab.shape[1]), F32),
                        pltpu.VMEM((L, 2 * LANES), BF16),
                        pltpu.VMEM((L, R_KV + LANES), BF16),
                        pltpu.VMEM((R_KV + 16, L), BF16),
                        pltpu.VMEM((L, tq), jnp.int32),
                        pltpu.VMEM((8, tq), jnp.int32),
                        pltpu.VMEM((8, tq), jnp.int32),
                        pltpu.VMEM((H_B, 8, tq), F32),
                        pltpu.VMEM((H_B, R_KV + 16, tq), F32),
                        pltpu.VMEM((DSA_KC, tq), F32),
                        pltpu.VMEM((DSA_KC, tq), F32),
                        pltpu.VMEM((DSA_KC, tq), F32),
                        pltpu.VMEM((DSA_KC, tq), BF16),
                        pltpu.VMEM((DSA_KC, tq), BF16),
                        pltpu.VMEM((H_B, tq, R_KV), BF16),
                        pltpu.VMEM((H_IDX, tq, 2 * LANES), BF16),
                        pltpu.VMEM((H_B, 8, tq), F32)],
        compiler_params=_cparams("arbitrary", "arbitrary"),
        name="dsa_mixer",
    )(qb, qi, gates, ki, ckv, wuk, wuv, ftab)


def _router_gates(lg):
    lane = lax.broadcasted_iota(jnp.int32, lg.shape, 1)
    ninf = jnp.float32(-jnp.inf)
    lc = jnp.where(lane < N_GROUPS, lg, ninf)
    mc = jnp.max(lc, axis=1, keepdims=True)
    gidx = jnp.min(jnp.where(lc == mc, lane, LANES), axis=1, keepdims=True)
    p_g = 1.0 / jnp.sum(jnp.exp(lc - mc), axis=1, keepdims=True)
    lo = N_GROUPS + E_PER_GROUP * gidx
    lf = jnp.where((lane >= lo) & (lane < lo + E_PER_GROUP), lg, ninf)
    v1 = jnp.max(lf, axis=1, keepdims=True)
    i1 = jnp.min(jnp.where(lf == v1, lane, LANES), axis=1, keepdims=True)
    lf2 = jnp.where(lane == i1, ninf, lf)
    v2 = jnp.max(lf2, axis=1, keepdims=True)
    i2 = jnp.min(jnp.where(lf2 == v2, lane, LANES), axis=1, keepdims=True)
    e2 = jnp.exp(v2 - v1)
    p1 = 1.0 / (1.0 + e2)
    p2 = e2 / (1.0 + e2)
    return p_g * jnp.where(lane == i1, p1, jnp.where(lane == i2, p2, 0.0))


def _moe_dense_kernel(x_ref, g_ref, sc_ref, sh_ref, gt_ref, wr_ref, br_ref, wg_ref, wu_ref, wd_ref, gf_ref,
                      o_ref, hf, gates, acc, *, final):
    e = pl.program_id(2)

    @pl.when(e == 0)
    def _():
        h = _modulated_norm(x_ref[0], g_ref[...], sc_ref[0], sh_ref[0])
        hf[...] = h.astype(BF16)
        gates[...] = _router_gates(_fdot(h, wr_ref[...]) + br_ref[...])
        acc[...] = jnp.zeros(acc.shape, F32)

    lane = lax.broadcasted_iota(jnp.int32, gates.shape, 1)
    gcol = jnp.sum(jnp.where(lane == N_GROUPS + e, gates[...], 0.0), axis=1, keepdims=True)
    hb = hf[...]
    a = jnp.dot(hb, wg_ref[0], preferred_element_type=F32)
    a = a * jax.nn.sigmoid(a) * jnp.dot(hb, wu_ref[0], preferred_element_type=F32) * gcol
    acc[...] += _bdot(a, wd_ref[0])

    @pl.when(e == pl.num_programs(2) - 1)
    def _():
        y = x_ref[0] + gt_ref[0] * acc[...]
        if final:
            y = y * lax.rsqrt(jnp.mean(y * y, axis=-1, keepdims=True) + EPS) * gf_ref[...]
        o_ref[0] = y


def moe_layer(x, g, sc, sh, gt, w_coarse, b_coarse, w_fine, b_fine, w_gate, w_up, w_down, g_final=None, tm=1024):
    bn, L, d = x.shape
    ne = N_GROUPS * E_PER_GROUP
    ff = w_gate.shape[-1]
    wr = jnp.concatenate([w_coarse, jnp.transpose(w_fine, (1, 0, 2)).reshape(d, ne)], axis=1)
    wr = _pad_cols(wr, LANES)
    br = _pad_cols(jnp.concatenate([b_coarse, b_fine.reshape(-1)])[None, :], LANES)
    wg = w_gate.reshape(ne, d, ff).astype(BF16)
    wu = w_up.reshape(ne, d, ff).astype(BF16)
    wd = w_down.reshape(ne, ff, d).astype(BF16)
    final = g_final is not None
    gf = (g_final if final else jnp.ones((d,), F32)).reshape(1, d)
    row = lambda b, i, e: (b, i, 0)
    per_b = lambda b, i, e: (b, 0, 0)
    const = lambda b, i, e: (0, 0)
    per_e = lambda b, i, e: (e, 0, 0)
    kern = functools.partial(_moe_dense_kernel, final=final)
    return pl.pallas_call(
        kern,
        grid=(bn, L // tm, ne),
        in_specs=[pl.BlockSpec((1, tm, d), row),
                  pl.BlockSpec((1, d), const),
                  pl.BlockSpec((1, 1, d), per_b),
                  pl.BlockSpec((1, 1, d), per_b),
                  pl.BlockSpec((1, 1, d), per_b),
                  pl.BlockSpec((d, LANES), const),
                  pl.BlockSpec((1, LANES), const),
                  pl.BlockSpec((1, d, ff), per_e),
                  pl.BlockSpec((1, d, ff), per_e),
                  pl.BlockSpec((1, ff, d), per_e),
                  pl.BlockSpec((1, d), const)],
        out_specs=pl.BlockSpec((1, tm, d), row),
        out_shape=jax.ShapeDtypeStruct((bn, L, d), F32),
        scratch_shapes=[pltpu.VMEM((tm, d), BF16), pltpu.VMEM((tm, LANES), F32), pltpu.VMEM((tm, d), F32)],
        compiler_params=_cparams("parallel", "parallel", "arbitrary"),
        name="moe_layer",
    )(x, g.reshape(1, d), sc.reshape(bn, 1, d), sh.reshape(bn, 1, d), gt.reshape(bn, 1, d), wr, br, wg, wu, wd, gf)


def _pad_cols(w, width):
    return jnp.pad(w, ((0, 0), (0, width - w.shape[1])))


def _segments(widths):
    offs = np.cumsum((0,) + tuple(widths))
    return tuple((int(o), int(w)) for o, w in zip(offs[:-1], widths))


AB_WIDTHS = (2 * H_A * DK_A, H_A * DV_A, H_A * DV_A, LANES, H_B * DH_B, R_KV, H_IDX * LANES, LANES)
AB_DTYPES = (F32, BF16, F32, F32, BF16, BF16, F32, F32)
CD_WIDTHS = (2 * H_C * DK_C, H_C * DV_C, H_C * DV_C, LANES, H_D * DH_D, H_D * DH_D, H_D * DH_D)
CD_DTYPES = (F32, BF16, F32, F32, BF16, BF16, BF16)


def pack_ab_w_in(w):
    d = w.shape[0]
    nqk = 2 * H_A * DK_A
    nv = H_A * DV_A
    o_i = nqk + 2 * nv
    o_qb = o_i + 2 * H_A
    o_ckv = o_qb + H_B * DH_B
    o_qi = o_ckv + R_KV
    o_ki = o_qi + H_IDX * D_IDX
    o_wi = o_ki + D_IDX
    gates = _pad_cols(jnp.concatenate([w[:, o_i:o_qb], w[:, o_wi:o_wi + H_IDX]], axis=1), LANES)
    wqi = w[:, o_qi:o_ki].reshape(d, H_IDX, D_IDX)
    wqi = jnp.concatenate([wqi, wqi], axis=-1).reshape(d, H_IDX * LANES)
    wki = w[:, o_ki:o_wi]
    return jnp.concatenate([w[:, :o_i], gates, w[:, o_qb:o_ckv], w[:, o_ckv:o_qi], wqi, wki, wki],
                           axis=1).astype(BF16)


def pack_cd_w_in(w):
    o_g = 2 * H_C * DK_C + H_C * DV_C
    o_r = o_g + GLA_RANK
    o_q = o_r + H_C * DV_C
    return jnp.concatenate([w[:, :o_g], w[:, o_r:o_q], _pad_cols(w[:, o_g:o_r], LANES), w[:, o_q:]],
                           axis=1).astype(BF16)


def kernel(x, c, w_ada, b_ada, g_mix, g_ffn, g_final, rel_bias, ab_w_in, ab_conv_w, ab_conv_b, ab_gate_b, ab_hnorm_g, ab_w_uk, ab_w_uv, ab_w_out, cd_w_in, cd_w_alpha, cd_b_alpha, cd_hnorm_g, cd_w_out, moe_w_coarse, moe_b_coarse, moe_w_fine, moe_b_fine, moe_w_gate, moe_w_up, moe_w_down):
    depth = w_ada.shape[0]
    L, d = x.shape[1], x.shape[2]
    mod = ada_mod(c, w_ada, b_ada)
    table = bias_table(rel_bias, L)
    for l in range(depth):
        sh_m, sc_m, gt_m, sh_f, sc_f, gt_f = [mod[l, :, i * d:(i + 1) * d] for i in range(6)]
        j = l // 2
        if l % 2 == 0:
            qk, va, oa, gates, qb, ckv, qi, ki = norm_proj(
                x, g_mix[l], sc_m, sh_m, pack_ab_w_in(ab_w_in[j]), _segments(AB_WIDTHS), AB_DTYPES)
            y1 = mlstm_mixer(qk, va, oa, gates, ab_conv_w[j], ab_conv_b[j], ab_gate_b[j], ab_hnorm_g[j])
            y2 = dsa_mixer(qb, qi, gates, ki, ckv, ab_w_uk[j], ab_w_uv[j], table)
            w_out = ab_w_out[j]
        else:
            qk, vc, rc, gc, qd, kd, vd = norm_proj(
                x, g_mix[l], sc_m, sh_m, pack_cd_w_in(cd_w_in[j]), _segments(CD_WIDTHS), CD_DTYPES)
            y1 = gla_mixer(qk, vc, gc, rc, cd_w_alpha[j], cd_b_alpha[j], cd_hnorm_g[j])
            y2 = dilated_mixer(qd, kd, vd, rel_bias)
            w_out = cd_w_out[j]
        x = outproj_residual(x, y1, y2, gt_m, w_out.astype(BF16))
        x = moe_layer(x, g_ffn[l], sc_f, sh_f, gt_f, moe_w_coarse[l], moe_b_coarse[l], moe_w_fine[l],
                      moe_b_fine[l], moe_w_gate[l], moe_w_up[l], moe_w_down[l],
                      g_final=g_final if l == depth - 1 else None)
    return x
```

```python
import functools
import math

import numpy as np
import jax
import jax.numpy as jnp
from jax import lax
from jax.experimental import pallas as pl
from jax.experimental.pallas import tpu as pltpu

F32 = jnp.float32
BF16 = jnp.bfloat16

EPS = 1e-6
NEG_INF = -1e30

H_A, DK_A, DV_A, CONV_A, MLSTM_CHUNK = 4, 64, 128, 4, 64
H_B, DH_B, R_KV, H_IDX, D_IDX = 8, 64, 128, 4, 64
TOPK_MAX, TOPK_DIV = 256, 4
H_C, DK_C, DV_C, GLA_RANK, GLA_TAU, GLA_CHUNK = 4, 64, 128, 16, 16.0, 64
H_D, DH_D = 8, 64
DIL_PATTERNS = ((128, 1), (512, 4), (2048, 16))
N_BUCKETS, MAX_DIST = 32, 2048
N_GROUPS, E_PER_GROUP, EXPERT_FF = 4, 4, 512

LANES = 128
VMEM_LIMIT = 56 * 1024 * 1024


def _cparams(*sem):
    return pltpu.CompilerParams(dimension_semantics=sem, vmem_limit_bytes=VMEM_LIMIT)


def _bdot(a, b):
    return jnp.dot(a.astype(BF16), b.astype(BF16), preferred_element_type=F32)


def _bdot_nt(a, b):
    return lax.dot_general(a.astype(BF16), b.astype(BF16), (((1,), (1,)), ((), ())),
                           preferred_element_type=F32)


def _fdot(a, b):
    return jnp.dot(a, b, preferred_element_type=F32, precision=lax.Precision.HIGHEST)


def _ada_kernel(c_ref, w_ref, b_ref, o_ref):
    c = c_ref[...]
    a = c * jax.nn.sigmoid(c)
    o_ref[0] = _fdot(a, w_ref[0]) + b_ref[0]


def ada_mod(c, w_ada, b_ada):
    depth, d, e = w_ada.shape
    bn = c.shape[0]
    tn = 1536
    return pl.pallas_call(
        _ada_kernel,
        grid=(depth, e // tn),
        in_specs=[pl.BlockSpec((bn, d), lambda l, j: (0, 0)),
                  pl.BlockSpec((1, d, tn), lambda l, j: (l, 0, j)),
                  pl.BlockSpec((1, 1, tn), lambda l, j: (l, 0, j))],
        out_specs=pl.BlockSpec((1, bn, tn), lambda l, j: (l, 0, j)),
        out_shape=jax.ShapeDtypeStruct((depth, bn, e), F32),
        compiler_params=_cparams("arbitrary", "arbitrary"),
        name="ada_mod",
    )(c, w_ada, b_ada.reshape(depth, 1, e))


def _modulated_norm(x, g, sc, sh):
    y = x * lax.rsqrt(jnp.mean(x * x, axis=-1, keepdims=True) + EPS) * g
    return y * (1.0 + sc) + sh


def _norm_proj_kernel(x_ref, g_ref, sc_ref, sh_ref, w_ref, *o_refs, segs):
    h = _modulated_norm(x_ref[0], g_ref[...], sc_ref[0], sh_ref[0]).astype(BF16)
    for o_ref, (start, width) in zip(o_refs, segs):
        o_ref[0] = jnp.dot(h, w_ref[:, start:start + width], preferred_element_type=F32).astype(o_ref.dtype)


def norm_proj(x, g, sc, sh, w, segs, dtypes, tm=512):
    bn, L, d = x.shape
    wp = w.shape[1]
    kern = functools.partial(_norm_proj_kernel, segs=tuple(segs))
    return pl.pallas_call(
        kern,
        grid=(bn, L // tm),
        in_specs=[pl.BlockSpec((1, tm, d), lambda b, i: (b, i, 0)),
                  pl.BlockSpec((1, d), lambda b, i: (0, 0)),
                  pl.BlockSpec((1, 1, d), lambda b, i: (b, 0, 0)),
                  pl.BlockSpec((1, 1, d), lambda b, i: (b, 0, 0)),
                  pl.BlockSpec((d, wp), lambda b, i: (0, 0))],
        out_specs=[pl.BlockSpec((1, tm, wd), lambda b, i: (b, i, 0)) for (_, wd) in segs],
        out_shape=[jax.ShapeDtypeStruct((bn, L, wd), dt) for (_, wd), dt in zip(segs, dtypes)],
        compiler_params=_cparams("parallel", "parallel"),
        name="norm_proj",
    )(x, g.reshape(1, d), sc.reshape(bn, 1, d), sh.reshape(bn, 1, d), w)


def _outproj_kernel(x_ref, ya_ref, yb_ref, gt_ref, w_ref, o_ref):
    na = ya_ref.shape[-1]
    y = _bdot(ya_ref[0], w_ref[:na, :]) + _bdot(yb_ref[0], w_ref[na:, :])
    o_ref[0] = x_ref[0] + gt_ref[0] * y


def outproj_residual(x, ya, yb, gt, w, tm=512):
    bn, L, d = x.shape
    na, nb = ya.shape[-1], yb.shape[-1]
    return pl.pallas_call(
        _outproj_kernel,
        grid=(bn, L // tm),
        in_specs=[pl.BlockSpec((1, tm, d), lambda b, i: (b, i, 0)),
                  pl.BlockSpec((1, tm, na), lambda b, i: (b, i, 0)),
                  pl.BlockSpec((1, tm, nb), lambda b, i: (b, i, 0)),
                  pl.BlockSpec((1, 1, d), lambda b, i: (b, 0, 0)),
                  pl.BlockSpec((na + nb, d), lambda b, i: (0, 0))],
        out_specs=pl.BlockSpec((1, tm, d), lambda b, i: (b, i, 0)),
        out_shape=jax.ShapeDtypeStruct((bn, L, d), F32),
        compiler_params=_cparams("parallel", "parallel"),
        name="outproj_residual",
    )(x, ya, yb, gt.reshape(bn, 1, d), w)


def _log_sigmoid(x):
    return jnp.minimum(x, 0.0) - jnp.log1p(jnp.exp(-jnp.abs(x)))


def _dot_tn(a, b):
    return lax.dot_general(a.astype(BF16), b.astype(BF16), (((0,), (0,)), ((), ())),
                           preferred_element_type=F32)


def _tri(T):
    r = np.arange(T)
    return (r[:, None] >= r[None, :]).astype(np.float32)


def _mlstm_kernel(qk_ref, v_ref, og_ref, gt_ref, cw_ref, cb_ref, gb_ref, hg_ref, ltri_ref, utri_ref,
                  o_ref, xbuf, cext, mstate):
    T = qk_ref.shape[1]
    tail = xbuf.shape[0] - T

    @pl.when(pl.program_id(1) == 0)
    def _():
        xbuf[0:tail, :] = jnp.zeros((tail, xbuf.shape[1]), F32)
        cext[...] = jnp.zeros(cext.shape, F32)
        mstate[...] = jnp.full(mstate.shape, NEG_INF, F32)

    x = qk_ref[0]
    xbuf[tail:tail + T, :] = x
    conv = cb_ref[...] + sum(xbuf[pl.ds(tail - (CONV_A - 1) + j, T), :] * cw_ref[j:j + 1, :]
                             for j in range(CONV_A))
    xbuf[0:tail, :] = x[T - tail:T, :]
    qk = conv * jax.nn.sigmoid(conv)

    G = gt_ref[0] + gb_ref[...]
    GT = G.T
    bcol = _fdot(ltri_ref[...], _log_sigmoid(G))
    brow = _fdot(_log_sigmoid(GT), utri_ref[...])
    causal = ltri_ref[...] > 0.5
    lane = lax.broadcasted_iota(jnp.int32, (T, DV_A), 1)
    ones_blk = jnp.where(lane == 0, 1.0, 0.0).astype(F32)
    nq = H_A * DK_A

    for h in range(H_A):
        b_c = bcol[:, H_A + h:H_A + h + 1]
        b_r = brow[H_A + h:H_A + h + 1, :]
        li_r = GT[h:h + 1, :]
        li_c = G[:, h:h + 1]
        m_prev = mstate[h:h + 1, 0:1]
        Dm = jnp.where(causal, b_c - b_r + li_r, NEG_INF)
        g = b_c + m_prev
        mt = jnp.maximum(g, jnp.max(Dm, axis=1, keepdims=True))
        w_intra = jnp.exp(Dm - mt)
        w_state = jnp.exp(g - mt)
        q = qk[:, h * DK_A:(h + 1) * DK_A] * (DK_A ** -0.5)
        k = qk[:, nq + h * DK_A:nq + (h + 1) * DK_A]
        vext = jnp.concatenate([v_ref[0, :, h * DV_A:(h + 1) * DV_A].astype(F32), ones_blk], axis=1)
        s = _bdot_nt(q, k) * w_intra
        nd = w_state * _bdot(q, cext[h]) + _bdot(s, vext)
        num = nd[:, :DV_A]
        den = nd[:, DV_A:DV_A + 1]
        hout = num / jnp.maximum(jnp.abs(den), jnp.exp(-mt))
        bL = b_c[T - 1:T, :]
        d_end = bL - b_c + li_c
        m_new = jnp.maximum(bL + m_prev, jnp.max(d_end, axis=0, keepdims=True))
        w_e = jnp.exp(d_end - m_new)
        decay = jnp.exp(bL + m_prev - m_new)
        cext[h] = decay * cext[h] + _dot_tn(k * w_e, vext)
        mstate[h:h + 1, :] = jnp.broadcast_to(m_new, (1, mstate.shape[1]))
        y = hout * lax.rsqrt(jnp.mean(hout * hout, axis=-1, keepdims=True) + EPS)
        y = y * hg_ref[:, h * DV_A:(h + 1) * DV_A]
        o_ref[0, :, h * DV_A:(h + 1) * DV_A] = (
            y * jax.nn.sigmoid(og_ref[0, :, h * DV_A:(h + 1) * DV_A])).astype(o_ref.dtype)


def mlstm_mixer(qk, v, og, gates, conv_w, conv_b, gate_b, hnorm_g, out_dtype=F32):
    bn, L, _ = qk.shape
    T = MLSTM_CHUNK
    nv = H_A * DV_A
    gb = jnp.zeros((1, LANES), F32).at[0, :2 * H_A].set(gate_b.reshape(-1))
    ltri = jnp.asarray(_tri(T))
    row = lambda b, i: (b, i, 0)
    const = lambda b, i: (0, 0)
    return pl.pallas_call(
        _mlstm_kernel,
        grid=(bn, L // T),
        in_specs=[pl.BlockSpec((1, T, qk.shape[-1]), row),
                  pl.BlockSpec((1, T, nv), row),
                  pl.BlockSpec((1, T, nv), row),
                  pl.BlockSpec((1, T, LANES), row),
                  pl.BlockSpec(conv_w.shape, const),
                  pl.BlockSpec((1, conv_b.shape[-1]), const),
                  pl.BlockSpec((1, LANES), const),
                  pl.BlockSpec((1, nv), const),
                  pl.BlockSpec((T, T), const),
                  pl.BlockSpec((T, T), const)],
        out_specs=pl.BlockSpec((1, T, nv), row),
        out_shape=jax.ShapeDtypeStruct((bn, L, nv), out_dtype),
        scratch_shapes=[pltpu.VMEM((8 + T, qk.shape[-1]), F32),
                        pltpu.VMEM((H_A, DK_A, 2 * DV_A), F32),
                        pltpu.VMEM((8, LANES), F32)],
        compiler_params=_cparams("parallel", "arbitrary"),
        name="mlstm_mixer",
    )(qk, v, og, gates, conv_w, conv_b.reshape(1, -1), gb, hnorm_g.reshape(1, -1), ltri, ltri.T)


def _gla_level_consts(T):
    r = np.arange(T)
    mq, mk, bm = [], [], []
    n = T
    while n >= 2:
        start = (r // n) * n
        mid = start + n // 2
        upper = r >= mid
        mq.append((upper[:, None] & (r[None, :] >= mid[:, None]) & (r[None, :] <= r[:, None])).astype(np.float32))
        mk.append((~upper[:, None] & (r[None, :] > r[:, None]) & (r[None, :] < mid[:, None])).astype(np.float32))
        bm.append((upper[:, None] & ~upper[None, :] & (start[:, None] == start[None, :])).astype(np.float32))
        n //= 2
    bm.append(np.eye(T, dtype=np.float32))
    return np.concatenate(mq, 0), np.concatenate(mk, 0), np.stack(bm, 0)


def _gla_kernel(qk_ref, v_ref, gc_ref, rc_ref, wa_ref, ba_ref, hg_ref, ltri_ref, mq_ref, mk_ref, bm_ref,
                o_ref, st):
    T = qk_ref.shape[1]
    nlev = bm_ref.shape[0] - 1
    nq = H_C * DK_C

    @pl.when(pl.program_id(1) == 0)
    def _():
        st[...] = jnp.zeros(st.shape, F32)

    qk = qk_ref[0]
    q = qk[:, :nq] * (DK_C ** -0.5)
    k = qk[:, nq:]
    la = _log_sigmoid(_bdot(gc_ref[0], wa_ref[...]) + ba_ref[...]) * (1.0 / GLA_TAU)
    eq = jnp.exp(_fdot(mq_ref[...], la))
    ek = jnp.exp(_fdot(mk_ref[...], la))
    bfull = _fdot(ltri_ref[...], la)
    bL = bfull[T - 1:T, :]
    qe = q * jnp.exp(bfull)
    kd = k * jnp.exp(bL - bfull)
    eL = jnp.exp(bL)

    for h in range(H_C):
        cs = slice(h * DK_C, (h + 1) * DK_C)
        vs = slice(h * DV_C, (h + 1) * DV_C)
        qh, kh, vh = q[:, cs], k[:, cs], v_ref[0, :, vs]
        A = bm_ref[nlev] * _bdot_nt(qh, kh)
        for lv in range(nlev):
            A = A + bm_ref[lv] * _bdot_nt(qh * eq[lv * T:(lv + 1) * T, cs], kh * ek[lv * T:(lv + 1) * T, cs])
        o = _bdot_nt(qe[:, cs], st[h]) + _bdot(A, vh)
        st[h] = eL[:, cs] * st[h] + _dot_tn(vh, kd[:, cs])
        y = o * lax.rsqrt(jnp.mean(o * o, axis=-1, keepdims=True) + EPS) * hg_ref[:, vs]
        r = rc_ref[0, :, vs]
        o_ref[0, :, vs] = (y * (r * jax.nn.sigmoid(r))).astype(o_ref.dtype)


def gla_mixer(qk, v, gc, rc, w_alpha, b_alpha, hnorm_g, out_dtype=F32):
    bn, L, _ = qk.shape
    T = GLA_CHUNK
    nv = H_C * DV_C
    nq = H_C * DK_C
    wa = jnp.pad(w_alpha, ((0, gc.shape[-1] - w_alpha.shape[0]), (0, 0))).astype(BF16)
    mq, mk, bm = _gla_level_consts(T)
    row = lambda b, i: (b, i, 0)
    const = lambda b, i: (0, 0)
    return pl.pallas_call(
        _gla_kernel,
        grid=(bn, L // T),
        in_specs=[pl.BlockSpec((1, T, 2 * nq), row),
                  pl.BlockSpec((1, T, nv), row),
                  pl.BlockSpec((1, T, gc.shape[-1]), row),
                  pl.BlockSpec((1, T, nv), row),
                  pl.BlockSpec(wa.shape, const),
                  pl.BlockSpec((1, nq), const),
                  pl.BlockSpec((1, nv), const),
                  pl.BlockSpec((T, T), const),
                  pl.BlockSpec(mq.shape, const),
                  pl.BlockSpec(mk.shape, const),
                  pl.BlockSpec(bm.shape, lambda b, i: (0, 0, 0))],
        out_specs=pl.BlockSpec((1, T, nv), row),
        out_shape=jax.ShapeDtypeStruct((bn, L, nv), out_dtype),
        scratch_shapes=[pltpu.VMEM((H_C, DV_C, DK_C), F32)],
        compiler_params=_cparams("parallel", "arbitrary"),
        name="gla_mixer",
    )(qk, v, gc, rc, wa, b_alpha.reshape(1, nq), hnorm_g.reshape(1, nv), jnp.asarray(_tri(T)),
      jnp.asarray(mq), jnp.asarray(mk), jnp.asarray(bm))


def _rel_bucket(dist):
    n = jnp.maximum(dist, 0)
    max_exact = N_BUCKETS // 2
    nf = jnp.maximum(n, 1).astype(F32)
    large = max_exact + (jnp.log(nf / max_exact) / math.log(MAX_DIST / max_exact)
                         * (N_BUCKETS - max_exact)).astype(jnp.int32)
    large = jnp.minimum(large, N_BUCKETS - 1)
    return jnp.where(n < max_exact, n, large)


def _bias_lookup_kernel(bk_ref, rb_ref, o_ref):
    n = bk_ref.shape[0]
    onehot = (lax.broadcasted_iota(jnp.int32, (n, LANES), 1) == bk_ref[...]).astype(F32)
    o_ref[...] = _fdot(onehot, rb_ref[...])


def bias_lookup(rel_bias, dist):
    n = dist.shape[0]
    tn = min(n, 4096)
    bk = jnp.broadcast_to(_rel_bucket(dist)[:, None], (n, LANES))
    rb = jnp.pad(rel_bias, ((0, LANES - rel_bias.shape[0]), (0, LANES - rel_bias.shape[1])))
    return pl.pallas_call(
        _bias_lookup_kernel,
        grid=(n // tn,),
        in_specs=[pl.BlockSpec((tn, LANES), lambda i: (i, 0)), pl.BlockSpec((LANES, LANES), lambda i: (0, 0))],
        out_specs=pl.BlockSpec((tn, LANES), lambda i: (i, 0)),
        out_shape=jax.ShapeDtypeStruct((n, LANES), F32),
        compiler_params=_cparams("parallel"),
        name="bias_lookup",
    )(bk, rb)


def bias_table(rel_bias, L):
    return bias_lookup(rel_bias, jnp.arange(L, dtype=jnp.int32))


def _dilated_kernel(*refs, has_prev, last):
    q_ref, kp_ref, kc_ref, vp_ref, vc_ref, bias_ref = refs[:6]
    prev_refs = refs[6:9] if has_prev else None
    out_refs = refs[9:] if has_prev else refs[6:]
    W = q_ref.shape[1]
    first = pl.program_id(2) == 0
    col = lax.broadcasted_iota(jnp.int32, (1, 2 * W), 1)
    edge = jnp.where((col < W) & first, NEG_INF, 0.0).astype(F32)
    for h in range(H_D):
        cs = slice(h * DH_D, (h + 1) * DH_D)
        qh = q_ref[0, :, cs] * (DH_D ** -0.5)
        kcat = jnp.concatenate([kp_ref[0, :, cs], kc_ref[0, :, cs]], axis=0)
        vcat = jnp.concatenate([vp_ref[0, :, cs], vc_ref[0, :, cs]], axis=0)
        logits = _bdot_nt(qh, kcat) + bias_ref[h] + edge
        m = jnp.max(logits, axis=1, keepdims=True)
        p = jnp.exp(logits - m)
        num = _bdot(p, vcat)
        m = jnp.broadcast_to(m, (W, DH_D))
        den = jnp.broadcast_to(jnp.sum(p, axis=1, keepdims=True), (W, DH_D))
        if has_prev:
            m0 = prev_refs[1][0, :, cs]
            mm = jnp.maximum(m, m0)
            a, b = jnp.exp(m - mm), jnp.exp(m0 - mm)
            num = num * a + prev_refs[0][0, :, cs] * b
            den = den * a + prev_refs[2][0, :, cs] * b
            m = mm
        if last:
            out_refs[0][0, :, cs] = (num / den).astype(out_refs[0].dtype)
        else:
            out_refs[0][0, :, cs] = num
            out_refs[1][0, :, cs] = m
            out_refs[2][0, :, cs] = den


def dilated_pattern(q, k, v, rel_bias, window, dil, prev, last, out_dtype):
    bn, L, nd = q.shape
    W = window // dil
    M = L // dil
    sub = lambda a: a.reshape(bn, M, dil * nd)
    tap = np.arange(W)[:, None] + W - np.arange(2 * W)[None, :]
    ok = (tap >= 0) & (tap <= W)
    tile = bias_lookup(rel_bias, jnp.asarray((np.clip(tap, 0, W) * dil).reshape(-1), jnp.int32))
    bias = jnp.where(jnp.asarray(ok)[None], tile[:, :H_D].reshape(W, 2 * W, H_D).transpose(2, 0, 1), NEG_INF)
    cur = lambda b, r, j: (b, j, r)
    prv = lambda b, r, j: (b, jnp.maximum(j - 1, 0), r)
    blk = (1, W, nd)
    has_prev = prev is not None
    n_out = 1 if last else 3
    outs = pl.pallas_call(
        functools.partial(_dilated_kernel, has_prev=has_prev, last=last),
        grid=(bn, dil, M // W),
        in_specs=[pl.BlockSpec(blk, cur), pl.BlockSpec(blk, prv), pl.BlockSpec(blk, cur),
                  pl.BlockSpec(blk, prv), pl.BlockSpec(blk, cur),
                  pl.BlockSpec((H_D, W, 2 * W), lambda b, r, j: (0, 0, 0))]
                 + [pl.BlockSpec(blk, cur)] * (3 if has_prev else 0),
        out_specs=[pl.BlockSpec(blk, cur)] * n_out,
        out_shape=[jax.ShapeDtypeStruct((bn, M, dil * nd), out_dtype if last else F32)] * n_out,
        compiler_params=_cparams("parallel", "parallel", "parallel"),
        name="dilated_pattern",
    )(sub(q), sub(k), sub(k), sub(v), sub(v), bias, *([sub(a) for a in prev] if has_prev else []))
    return [a.reshape(bn, L, nd) for a in outs]


def dilated_mixer(q, k, v, rel_bias, out_dtype=F32):
    state = None
    for idx, (window, dil) in enumerate(DIL_PATTERNS):
        state = dilated_pattern(q, k, v, rel_bias, window, dil, state, idx == len(DIL_PATTERNS) - 1, out_dtype)
    return state[0]


INT_MIN = -2 ** 31
DSA_TQ = 256
DSA_KC = 512
DSA_AR = 64


def _split_hi_lo(x):
    hi = x.astype(BF16)
    lo = (x - hi.astype(F32)).astype(BF16)
    return hi, lo


def _dsa_kernel(qb_ref, qi_ref, gt_ref, ki_ref, ckv_ref, wuk_ref, wuv_ref, ftab_ref, o_ref,
                skew, kcat, cke, cket, skey, thr_ref, j_ref, m_ref, acc_ref, madd_ref, lg0_ref, lg1_ref,
                p0_ref, p1_ref, qabs_ref, qcat_ref, alpha_ref, *, topk):
    tq = qb_ref.shape[1]
    L = ki_ref.shape[1]
    kc = DSA_KC
    ar = DSA_AR
    i = pl.program_id(1)
    t0 = i * tq
    nck = (t0 + tq + kc - 1) // kc
    left = lax.broadcasted_iota(jnp.int32, (1, LANES), 1) < (LANES // 2)
    tpos = t0 + lax.broadcasted_iota(jnp.int32, (1, tq), 1)
    krow = lax.broadcasted_iota(jnp.int32, (kc, tq), 0)

    @pl.when((pl.program_id(0) == 0) & (i == 0))
    def _():
        ltot = ftab_ref.shape[1]
        for h in range(H_B):
            x = jnp.broadcast_to(ftab_ref[h:h + 1, :], (LANES, ltot))
            skew[h] = pltpu.roll(x, ltot - LANES, 1, stride=1, stride_axis=0)

    @pl.when(i == 0)
    def _():
        hi, lo = _split_hi_lo(ki_ref[0])
        kcat[:, :LANES] = hi
        kcat[:, LANES:] = jnp.where(left, lo, jnp.zeros_like(lo))
        cke[:, :R_KV] = ckv_ref[0]
        one_col = lax.broadcasted_iota(jnp.int32, (L, LANES), 1) == 0
        cke[:, R_KV:] = jnp.where(one_col, 1.0, 0.0).astype(BF16)
        nr = cket.shape[0]
        eye = (lax.broadcasted_iota(jnp.int32, (nr, R_KV + LANES), 0)
               == lax.broadcasted_iota(jnp.int32, (nr, R_KV + LANES), 1)).astype(BF16)
        for c in range(L // kc):
            cket[:, c * kc:(c + 1) * kc] = _bdot_nt(eye, cke[c * kc:(c + 1) * kc, :]).astype(BF16)

    gt_t = gt_ref[0].T
    ws = []
    for h in range(H_IDX):
        hi, lo = _split_hi_lo(qi_ref[0, :, h * LANES:(h + 1) * LANES])
        qcat_ref[h, :, :LANES] = jnp.where(left, hi, lo)
        qcat_ref[h, :, LANES:] = jnp.where(left, hi, jnp.zeros_like(hi))
        ws.append(gt_t[2 * H_A + h:2 * H_A + h + 1, :] * (H_IDX ** -0.5))

    def score_body(c, carry):
        s0 = pl.multiple_of(c * kc, kc)
        sc = jnp.zeros((kc, tq), F32)
        for h in range(H_IDX):
            sc = sc + ws[h] * jnp.maximum(lax.dot_general(kcat[pl.ds(s0, kc), :], qcat_ref[h],
                                                          (((1,), (1,)), ((), ())),
                                                          preferred_element_type=F32), 0.0)
        bits = lax.bitcast_convert_type(sc + 0.0, jnp.int32)
        key = jnp.where(bits < 0, bits ^ jnp.int32(0x7FFFFFFF), bits)
        skey[pl.ds(s0, kc), :] = jnp.where(s0 + krow <= tpos, key, jnp.int32(INT_MIN))
        return carry

    lax.fori_loop(0, nck, score_body, 0)

    arow = lax.broadcasted_iota(jnp.int32, (ar, tq), 0)

    def count(pred):
        def body(c, acc):
            for a in range(kc // ar):
                r0 = pl.multiple_of(c * kc + a * ar, ar)
                acc = acc + pred(skey[pl.ds(r0, ar), :], r0 + arow).astype(jnp.int32)
            return acc
        acc = lax.fori_loop(0, nck, body, jnp.zeros((ar, tq), jnp.int32))
        return jnp.sum(acc, axis=0, keepdims=True)

    thr_ref[...] = jnp.full(thr_ref.shape, INT_MIN, jnp.int32)
    j_ref[...] = jnp.full(j_ref.shape, L, jnp.int32)

    @pl.when(t0 + tq > topk)
    def _():
        def bit_body(it, tu):
            cand_u = tu | (jnp.int32(1) << (31 - it))
            cand = cand_u ^ jnp.int32(INT_MIN)
            return jnp.where(count(lambda blk, idx: blk >= cand) >= topk, cand_u, tu)

        tu = lax.fori_loop(0, 32, bit_body, jnp.zeros((1, tq), jnp.int32))
        thr = tu ^ jnp.int32(INT_MIN)
        thr_ref[...] = jnp.broadcast_to(thr, thr_ref.shape)
        need = topk - count(lambda blk, idx: blk > thr)
        n_eq = count(lambda blk, idx: blk == thr)
        tied = ((n_eq > need) & (thr != INT_MIN)).astype(jnp.int32)

        @pl.when(jnp.max(tied) > 0)
        def _():
            nbits = max(1, (L - 1).bit_length())

            def jbit_body(it, jv):
                cand = jv | (jnp.int32(1) << (nbits - 1 - it))
                return jnp.where(count(lambda blk, idx: (blk == thr) & (idx < cand)) < need, cand, jv)

            jv = lax.fori_loop(0, nbits, jbit_body, jnp.zeros((1, tq), jnp.int32))
            j_ref[...] = jnp.broadcast_to(jv, j_ref.shape)

    thr = thr_ref[0:1, :]
    jv = j_ref[0:1, :]
    for h in range(H_B):
        qabs_ref[h] = (_bdot_nt(qb_ref[0, :, h * DH_B:(h + 1) * DH_B], wuk_ref[h]) * (DH_B ** -0.5)).astype(BF16)
    m_ref[...] = jnp.full(m_ref.shape, NEG_INF, F32)
    acc_ref[...] = jnp.zeros(acc_ref.shape, F32)

    sb = LANES
    nsb = kc // sb
    srow = lax.broadcasted_iota(jnp.int32, (sb, tq), 0)

    def attn_body(c, carry):
        s0 = pl.multiple_of(c * kc, kc)
        for a in range(nsb):
            blk = skey[pl.ds(s0 + a * sb, sb), :]
            idx = s0 + a * sb + srow
            sel = ((blk > thr) | ((blk == thr) & (idx <= jv))) & (idx <= tpos)
            madd_ref[a * sb:(a + 1) * sb, :] = jnp.where(sel, 0.0, NEG_INF).astype(F32)
        def scores(h, lg):
            lg[...] = _bdot_nt(cke[pl.ds(s0, kc), :R_KV], qabs_ref[h])

        def softmax(h, lg, pb):
            m_old = m_ref[h, 0:1, :]
            m_new = m_old
            for a in range(nsb):
                rows = slice(a * sb, (a + 1) * sb)
                off = pl.multiple_of(t0 - s0 + kc - (a + 1) * sb, sb)
                x = lg[rows, :] + skew[h, :, pl.ds(off, tq)] + madd_ref[rows, :]
                lg[rows, :] = x
                m_new = jnp.maximum(m_new, jnp.max(x, axis=0, keepdims=True))
            for a in range(nsb):
                rows = slice(a * sb, (a + 1) * sb)
                pb[rows, :] = jnp.exp(lg[rows, :] - m_new).astype(BF16)
            alpha_ref[h] = jnp.broadcast_to(jnp.exp(m_old - m_new), alpha_ref.shape[1:])
            m_ref[h] = jnp.broadcast_to(m_new, m_ref.shape[1:])

        def accumulate(h, pb):
            acc_ref[h] = alpha_ref[h, 0:1, :] * acc_ref[h] + jnp.dot(cket[:, pl.ds(s0, kc)], pb[...],
                                                                    preferred_element_type=F32)

        scores(0, lg0_ref)
        scores(1, lg1_ref)
        softmax(0, lg0_ref, p0_ref)
        scores(2, lg0_ref)
        accumulate(0, p0_ref)
        softmax(1, lg1_ref, p1_ref)

        def pair_body(k, carry2):
            e = 2 * k
            scores(e + 1, lg1_ref)
            accumulate(e - 1, p1_ref)
            softmax(e, lg0_ref, p0_ref)
            scores((e + 2) % H_B, lg0_ref)
            accumulate(e, p0_ref)
            softmax(e + 1, lg1_ref, p1_ref)
            return carry2

        lax.fori_loop(1, H_B // 2, pair_body, 0)
        accumulate(H_B - 1, p1_ref)
        return carry

    lax.fori_loop(0, nck, attn_body, 0)
    for h in range(H_B):
        o_lat = acc_ref[h, :R_KV, :] / acc_ref[h, R_KV:R_KV + 1, :]
        o_ref[0, :, h * DH_B:(h + 1) * DH_B] = _dot_tn(o_lat, wuv_ref[h]).astype(o_ref.dtype)


def dsa_mixer(qb, qi, gates, ki, ckv, w_uk, w_uv, table, out_dtype=F32):
    bn, L, _ = qb.shape
    tq = min(DSA_TQ, L)
    topk = min(TOPK_MAX, L // TOPK_DIV)
    wuk = jnp.transpose(w_uk, (1, 0, 2)).astype(BF16)
    wuv = jnp.transpose(w_uv, (1, 0, 2)).astype(BF16)
    ftab = jnp.concatenate([jnp.zeros((8, DSA_KC), F32), table[:, :8].T], axis=1)
    row = lambda b, i: (b, i, 0)
    full = lambda b, i: (b, 0, 0)
    c3 = lambda b, i: (0, 0, 0)
    kern = functools.partial(_dsa_kernel, topk=topk)
    return pl.pallas_call(
        kern,
        grid=(bn, L // tq),
        in_specs=[pl.BlockSpec((1, tq, H_B * DH_B), row),
                  pl.BlockSpec((1, tq, H_IDX * LANES), row),
                  pl.BlockSpec((1, tq, LANES), row),
                  pl.BlockSpec((1, L, LANES), full),
                  pl.BlockSpec((1, L, R_KV), full),
                  pl.BlockSpec(wuk.shape, c3),
                  pl.BlockSpec(wuv.shape, c3),
                  pl.BlockSpec(ftab.shape, lambda b, i: (0, 0))],
        out_specs=pl.BlockSpec((1, tq, H_B * DH_B), row),
        out_shape=jax.ShapeDtypeStruct((bn, L, H_B * DH_B), out_dtype),
        scratch_shapes=[pltpu.VMEM((H_B, LANES, ftab.shape[1]), F32),
                        pltpu.VMEM((L, 2 * LANES), BF16),
                        pltpu.VMEM((L, R_KV + LANES), BF16),
                        pltpu.VMEM((R_KV + 16, L), BF16),
                        pltpu.VMEM((L, tq), jnp.int32),
                        pltpu.VMEM((8, tq), jnp.int32),
                        pltpu.VMEM((8, tq), jnp.int32),
                        pltpu.VMEM((H_B, 8, tq), F32),
                        pltpu.VMEM((H_B, R_KV + 16, tq), F32),
                        pltpu.VMEM((DSA_KC, tq), F32),
                        pltpu.VMEM((DSA_KC, tq), F32),
                        pltpu.VMEM((DSA_KC, tq), F32),
                        pltpu.VMEM((DSA_KC, tq), BF16),
                        pltpu.VMEM((DSA_KC, tq), BF16),
                        pltpu.VMEM((H_B, tq, R_KV), BF16),
                        pltpu.VMEM((H_IDX, tq, 2 * LANES), BF16),
                        pltpu.VMEM((H_B, 8, tq), F32)],
        compiler_params=_cparams("arbitrary", "arbitrary"),
        name="dsa_mixer",
    )(qb, qi, gates, ki, ckv, wuk, wuv, ftab)


def _router_gates(lg):
    lane = lax.broadcasted_iota(jnp.int32, lg.shape, 1)
    ninf = jnp.float32(-jnp.inf)
    lc = jnp.where(lane < N_GROUPS, lg, ninf)
    mc = jnp.max(lc, axis=1, keepdims=True)
    gidx = jnp.min(jnp.where(lc == mc, lane, LANES), axis=1, keepdims=True)
    p_g = 1.0 / jnp.sum(jnp.exp(lc - mc), axis=1, keepdims=True)
    lo = N_GROUPS + E_PER_GROUP * gidx
    lf = jnp.where((lane >= lo) & (lane < lo + E_PER_GROUP), lg, ninf)
    v1 = jnp.max(lf, axis=1, keepdims=True)
    i1 = jnp.min(jnp.where(lf == v1, lane, LANES), axis=1, keepdims=True)
    lf2 = jnp.where(lane == i1, ninf, lf)
    v2 = jnp.max(lf2, axis=1, keepdims=True)
    i2 = jnp.min(jnp.where(lf2 == v2, lane, LANES), axis=1, keepdims=True)
    e2 = jnp.exp(v2 - v1)
    p1 = 1.0 / (1.0 + e2)
    p2 = e2 / (1.0 + e2)
    return p_g * jnp.where(lane == i1, p1, jnp.where(lane == i2, p2, 0.0))


def _moe_dense_kernel(x_ref, g_ref, sc_ref, sh_ref, gt_ref, wr_ref, br_ref, wg_ref, wu_ref, wd_ref, gf_ref,
                      o_ref, hf, gates, acc, *, final):
    e = pl.program_id(2)

    @pl.when(e == 0)
    def _():
        h = _modulated_norm(x_ref[0], g_ref[...], sc_ref[0], sh_ref[0])
        hf[...] = h.astype(BF16)
        gates[...] = _router_gates(_fdot(h, wr_ref[...]) + br_ref[...])
        acc[...] = jnp.zeros(acc.shape, F32)

    lane = lax.broadcasted_iota(jnp.int32, gates.shape, 1)
    gcol = jnp.sum(jnp.where(lane == N_GROUPS + e, gates[...], 0.0), axis=1, keepdims=True)
    hb = hf[...]
    a = jnp.dot(hb, wg_ref[0], preferred_element_type=F32)
    a = a * jax.nn.sigmoid(a) * jnp.dot(hb, wu_ref[0], preferred_element_type=F32) * gcol
    acc[...] += _bdot(a, wd_ref[0])

    @pl.when(e == pl.num_programs(2) - 1)
    def _():
        y = x_ref[0] + gt_ref[0] * acc[...]
        if final:
            y = y * lax.rsqrt(jnp.mean(y * y, axis=-1, keepdims=True) + EPS) * gf_ref[...]
        o_ref[0] = y


def moe_layer(x, g, sc, sh, gt, w_coarse, b_coarse, w_fine, b_fine, w_gate, w_up, w_down, g_final=None, tm=1024):
    bn, L, d = x.shape
    ne = N_GROUPS * E_PER_GROUP
    ff = w_gate.shape[-1]
    wr = jnp.concatenate([w_coarse, jnp.transpose(w_fine, (1, 0, 2)).reshape(d, ne)], axis=1)
    wr = _pad_cols(wr, LANES)
    br = _pad_cols(jnp.concatenate([b_coarse, b_fine.reshape(-1)])[None, :], LANES)
    wg = w_gate.reshape(ne, d, ff).astype(BF16)
    wu = w_up.reshape(ne, d, ff).astype(BF16)
    wd = w_down.reshape(ne, ff, d).astype(BF16)
    final = g_final is not None
    gf = (g_final if final else jnp.ones((d,), F32)).reshape(1, d)
    row = lambda b, i, e: (b, i, 0)
    per_b = lambda b, i, e: (b, 0, 0)
    const = lambda b, i, e: (0, 0)
    per_e = lambda b, i, e: (e, 0, 0)
    kern = functools.partial(_moe_dense_kernel, final=final)
    return pl.pallas_call(
        kern,
        grid=(bn, L // tm, ne),
        in_specs=[pl.BlockSpec((1, tm, d), row),
                  pl.BlockSpec((1, d), const),
                  pl.BlockSpec((1, 1, d), per_b),
                  pl.BlockSpec((1, 1, d), per_b),
                  pl.BlockSpec((1, 1, d), per_b),
                  pl.BlockSpec((d, LANES), const),
                  pl.BlockSpec((1, LANES), const),
                  pl.BlockSpec((1, d, ff), per_e),
                  pl.BlockSpec((1, d, ff), per_e),
                  pl.BlockSpec((1, ff, d), per_e),
                  pl.BlockSpec((1, d), const)],
        out_specs=pl.BlockSpec((1, tm, d), row),
        out_shape=jax.ShapeDtypeStruct((bn, L, d), F32),
        scratch_shapes=[pltpu.VMEM((tm, d), BF16), pltpu.VMEM((tm, LANES), F32), pltpu.VMEM((tm, d), F32)],
        compiler_params=_cparams("parallel", "parallel", "arbitrary"),
        name="moe_layer",
    )(x, g.reshape(1, d), sc.reshape(bn, 1, d), sh.reshape(bn, 1, d), gt.reshape(bn, 1, d), wr, br, wg, wu, wd, gf)


def _pad_cols(w, width):
    return jnp.pad(w, ((0, 0), (0, width - w.shape[1])))


def _segments(widths):
    offs = np.cumsum((0,) + tuple(widths))
    return tuple((int(o), int(w)) for o, w in zip(offs[:-1], widths))


AB_WIDTHS = (2 * H_A * DK_A, H_A * DV_A, H_A * DV_A, LANES, H_B * DH_B, R_KV, H_IDX * LANES, LANES)
AB_DTYPES = (F32, BF16, F32, F32, BF16, BF16, F32, F32)
CD_WIDTHS = (2 * H_C * DK_C, H_C * DV_C, H_C * DV_C, LANES, H_D * DH_D, H_D * DH_D, H_D * DH_D)
CD_DTYPES = (F32, BF16, F32, F32, BF16, BF16, BF16)


def pack_ab_w_in(w):
    d = w.shape[0]
    nqk = 2 * H_A * DK_A
    nv = H_A * DV_A
    o_i = nqk + 2 * nv
    o_qb = o_i + 2 * H_A
    o_ckv = o_qb + H_B * DH_B
    o_qi = o_ckv + R_KV
    o_ki = o_qi + H_IDX * D_IDX
    o_wi = o_ki + D_IDX
    gates = _pad_cols(jnp.concatenate([w[:, o_i:o_qb], w[:, o_wi:o_wi + H_IDX]], axis=1), LANES)
    wqi = w[:, o_qi:o_ki].reshape(d, H_IDX, D_IDX)
    wqi = jnp.concatenate([wqi, wqi], axis=-1).reshape(d, H_IDX * LANES)
    wki = w[:, o_ki:o_wi]
    return jnp.concatenate([w[:, :o_i], gates, w[:, o_qb:o_ckv], w[:, o_ckv:o_qi], wqi, wki, wki],
                           axis=1).astype(BF16)


def pack_cd_w_in(w):
    o_g = 2 * H_C * DK_C + H_C * DV_C
    o_r = o_g + GLA_RANK
    o_q = o_r + H_C * DV_C
    return jnp.concatenate([w[:, :o_g], w[:, o_r:o_q], _pad_cols(w[:, o_g:o_r], LANES), w[:, o_q:]],
                           axis=1).astype(BF16)


def kernel(x, c, w_ada, b_ada, g_mix, g_ffn, g_final, rel_bias, ab_w_in, ab_conv_w, ab_conv_b, ab_gate_b, ab_hnorm_g, ab_w_uk, ab_w_uv, ab_w_out, cd_w_in, cd_w_alpha, cd_b_alpha, cd_hnorm_g, cd_w_out, moe_w_coarse, moe_b_coarse, moe_w_fine, moe_b_fine, moe_w_gate, moe_w_up, moe_w_down):
    depth = w_ada.shape[0]
    L, d = x.shape[1], x.shape[2]
    mod = ada_mod(c, w_ada, b_ada)
    table = bias_table(rel_bias, L)
    for l in range(depth):
        sh_m, sc_m, gt_m, sh_f, sc_f, gt_f = [mod[l, :, i * d:(i + 1) * d] for i in range(6)]
        j = l // 2
        if l % 2 == 0:
            qk, va, oa, gates, qb, ckv, qi, ki = norm_proj(
                x, g_mix[l], sc_m, sh_m, pack_ab_w_in(ab_w_in[j]), _segments(AB_WIDTHS), AB_DTYPES)
            y1 = mlstm_mixer(qk, va, oa, gates, ab_conv_w[j], ab_conv_b[j], ab_gate_b[j], ab_hnorm_g[j],
                             out_dtype=BF16)
            y2 = dsa_mixer(qb, qi, gates, ki, ckv, ab_w_uk[j], ab_w_uv[j], table, out_dtype=BF16)
            w_out = ab_w_out[j]
        else:
            qk, vc, rc, gc, qd, kd, vd = norm_proj(
                x, g_mix[l], sc_m, sh_m, pack_cd_w_in(cd_w_in[j]), _segments(CD_WIDTHS), CD_DTYPES)
            y1 = gla_mixer(qk, vc, gc, rc, cd_w_alpha[j], cd_b_alpha[j], cd_hnorm_g[j], out_dtype=BF16)
            y2 = dilated_mixer(qd, kd, vd, rel_bias, out_dtype=BF16)
            w_out = cd_w_out[j]
        x = outproj_residual(x, y1, y2, gt_m, w_out.astype(BF16))
        x = moe_layer(x, g_ffn[l], sc_f, sh_f, gt_f, moe_w_coarse[l], moe_b_coarse[l], moe_w_fine[l],
                      moe_b_fine[l], moe_w_gate[l], moe_w_up[l], moe_w_down[l],
                      g_final=g_final if l == depth - 1 else None)
    return x
```

```python
import functools
import math

import numpy as np
import jax
import jax.numpy as jnp
from jax import lax
from jax.experimental import pallas as pl
from jax.experimental.pallas import tpu as pltpu

F32 = jnp.float32
BF16 = jnp.bfloat16

EPS = 1e-6
NEG_INF = -1e30

H_A, DK_A, DV_A, CONV_A, MLSTM_CHUNK = 4, 64, 128, 4, 64
H_B, DH_B, R_KV, H_IDX, D_IDX = 8, 64, 128, 4, 64
TOPK_MAX, TOPK_DIV = 256, 4
H_C, DK_C, DV_C, GLA_RANK, GLA_TAU, GLA_CHUNK = 4, 64, 128, 16, 16.0, 64
H_D, DH_D = 8, 64
DIL_PATTERNS = ((128, 1), (512, 4), (2048, 16))
N_BUCKETS, MAX_DIST = 32, 2048
N_GROUPS, E_PER_GROUP, EXPERT_FF = 4, 4, 512

LANES = 128
VMEM_LIMIT = 56 * 1024 * 1024


def _cparams(*sem):
    return pltpu.CompilerParams(dimension_semantics=sem, vmem_limit_bytes=VMEM_LIMIT)


def _bdot(a, b):
    return jnp.dot(a.astype(BF16), b.astype(BF16), preferred_element_type=F32)


def _bdot_nt(a, b):
    return lax.dot_general(a.astype(BF16), b.astype(BF16), (((1,), (1,)), ((), ())),
                           preferred_element_type=F32)


def _fdot(a, b):
    return jnp.dot(a, b, preferred_element_type=F32, precision=lax.Precision.HIGHEST)


def _ada_kernel(c_ref, w_ref, b_ref, o_ref):
    c = c_ref[...]
    a = c * jax.nn.sigmoid(c)
    o_ref[0] = _fdot(a, w_ref[0]) + b_ref[0]


def ada_mod(c, w_ada, b_ada):
    depth, d, e = w_ada.shape
    bn = c.shape[0]
    tn = 1536
    return pl.pallas_call(
        _ada_kernel,
        grid=(depth, e // tn),
        in_specs=[pl.BlockSpec((bn, d), lambda l, j: (0, 0)),
                  pl.BlockSpec((1, d, tn), lambda l, j: (l, 0, j)),
                  pl.BlockSpec((1, 1, tn), lambda l, j: (l, 0, j))],
        out_specs=pl.BlockSpec((1, bn, tn), lambda l, j: (l, 0, j)),
        out_shape=jax.ShapeDtypeStruct((depth, bn, e), F32),
        compiler_params=_cparams("arbitrary", "arbitrary"),
        name="ada_mod",
    )(c, w_ada, b_ada.reshape(depth, 1, e))


def _modulated_norm(x, g, sc, sh):
    y = x * lax.rsqrt(jnp.mean(x * x, axis=-1, keepdims=True) + EPS) * g
    return y * (1.0 + sc) + sh


def _norm_proj_kernel(x_ref, g_ref, sc_ref, sh_ref, w_ref, *o_refs, segs):
    h = _modulated_norm(x_ref[0], g_ref[...], sc_ref[0], sh_ref[0]).astype(BF16)
    for o_ref, (start, width) in zip(o_refs, segs):
        o_ref[0] = jnp.dot(h, w_ref[:, start:start + width], preferred_element_type=F32).astype(o_ref.dtype)


def norm_proj(x, g, sc, sh, w, segs, dtypes, tm=512):
    bn, L, d = x.shape
    wp = w.shape[1]
    kern = functools.partial(_norm_proj_kernel, segs=tuple(segs))
    return pl.pallas_call(
        kern,
        grid=(bn, L // tm),
        in_specs=[pl.BlockSpec((1, tm, d), lambda b, i: (b, i, 0)),
                  pl.BlockSpec((1, d), lambda b, i: (0, 0)),
                  pl.BlockSpec((1, 1, d), lambda b, i: (b, 0, 0)),
                  pl.BlockSpec((1, 1, d), lambda b, i: (b, 0, 0)),
                  pl.BlockSpec((d, wp), lambda b, i: (0, 0))],
        out_specs=[pl.BlockSpec((1, tm, wd), lambda b, i: (b, i, 0)) for (_, wd) in segs],
        out_shape=[jax.ShapeDtypeStruct((bn, L, wd), dt) for (_, wd), dt in zip(segs, dtypes)],
        compiler_params=_cparams("parallel", "parallel"),
        name="norm_proj",
    )(x, g.reshape(1, d), sc.reshape(bn, 1, d), sh.reshape(bn, 1, d), w)


def _outproj_kernel(x_ref, ya_ref, yb_ref, gt_ref, w_ref, o_ref):
    na = ya_ref.shape[-1]
    y = _bdot(ya_ref[0], w_ref[:na, :]) + _bdot(yb_ref[0], w_ref[na:, :])
    o_ref[0] = x_ref[0] + gt_ref[0] * y


def outproj_residual(x, ya, yb, gt, w, tm=512):
    bn, L, d = x.shape
    na, nb = ya.shape[-1], yb.shape[-1]
    return pl.pallas_call(
        _outproj_kernel,
        grid=(bn, L // tm),
        in_specs=[pl.BlockSpec((1, tm, d), lambda b, i: (b, i, 0)),
                  pl.BlockSpec((1, tm, na), lambda b, i: (b, i, 0)),
                  pl.BlockSpec((1, tm, nb), lambda b, i: (b, i, 0)),
                  pl.BlockSpec((1, 1, d), lambda b, i: (b, 0, 0)),
                  pl.BlockSpec((na + nb, d), lambda b, i: (0, 0))],
        out_specs=pl.BlockSpec((1, tm, d), lambda b, i: (b, i, 0)),
        out_shape=jax.ShapeDtypeStruct((bn, L, d), F32),
        compiler_params=_cparams("parallel", "parallel"),
        name="outproj_residual",
    )(x, ya, yb, gt.reshape(bn, 1, d), w)


def _log_sigmoid(x):
    return jnp.minimum(x, 0.0) - jnp.log1p(jnp.exp(-jnp.abs(x)))


def _dot_tn(a, b):
    return lax.dot_general(a.astype(BF16), b.astype(BF16), (((0,), (0,)), ((), ())),
                           preferred_element_type=F32)


def _tri(T):
    r = np.arange(T)
    return (r[:, None] >= r[None, :]).astype(np.float32)


def _mlstm_kernel(qk_ref, v_ref, og_ref, gt_ref, cw_ref, cb_ref, gb_ref, hg_ref, ltri_ref, utri_ref,
                  o_ref, xbuf, cext, mstate):
    T = qk_ref.shape[1]
    tail = xbuf.shape[0] - T

    @pl.when(pl.program_id(1) == 0)
    def _():
        xbuf[0:tail, :] = jnp.zeros((tail, xbuf.shape[1]), F32)
        cext[...] = jnp.zeros(cext.shape, F32)
        mstate[...] = jnp.full(mstate.shape, NEG_INF, F32)

    x = qk_ref[0]
    xbuf[tail:tail + T, :] = x
    conv = cb_ref[...] + sum(xbuf[pl.ds(tail - (CONV_A - 1) + j, T), :] * cw_ref[j:j + 1, :]
                             for j in range(CONV_A))
    xbuf[0:tail, :] = x[T - tail:T, :]
    qk = conv * jax.nn.sigmoid(conv)

    G = gt_ref[0] + gb_ref[...]
    GT = G.T
    bcol = _fdot(ltri_ref[...], _log_sigmoid(G))
    brow = _fdot(_log_sigmoid(GT), utri_ref[...])
    causal = ltri_ref[...] > 0.5
    lane = lax.broadcasted_iota(jnp.int32, (T, DV_A), 1)
    ones_blk = jnp.where(lane == 0, 1.0, 0.0).astype(F32)
    nq = H_A * DK_A

    for h in range(H_A):
        b_c = bcol[:, H_A + h:H_A + h + 1]
        b_r = brow[H_A + h:H_A + h + 1, :]
        li_r = GT[h:h + 1, :]
        li_c = G[:, h:h + 1]
        m_prev = mstate[h:h + 1, 0:1]
        Dm = jnp.where(causal, b_c - b_r + li_r, NEG_INF)
        g = b_c + m_prev
        mt = jnp.maximum(g, jnp.max(Dm, axis=1, keepdims=True))
        w_intra = jnp.exp(Dm - mt)
        w_state = jnp.exp(g - mt)
        q = qk[:, h * DK_A:(h + 1) * DK_A] * (DK_A ** -0.5)
        k = qk[:, nq + h * DK_A:nq + (h + 1) * DK_A]
        vext = jnp.concatenate([v_ref[0, :, h * DV_A:(h + 1) * DV_A].astype(F32), ones_blk], axis=1)
        s = _bdot_nt(q, k) * w_intra
        nd = w_state * _bdot(q, cext[h]) + _bdot(s, vext)
        num = nd[:, :DV_A]
        den = nd[:, DV_A:DV_A + 1]
        hout = num / jnp.maximum(jnp.abs(den), jnp.exp(-mt))
        bL = b_c[T - 1:T, :]
        d_end = bL - b_c + li_c
        m_new = jnp.maximum(bL + m_prev, jnp.max(d_end, axis=0, keepdims=True))
        w_e = jnp.exp(d_end - m_new)
        decay = jnp.exp(bL + m_prev - m_new)
        cext[h] = decay * cext[h] + _dot_tn(k * w_e, vext)
        mstate[h:h + 1, :] = jnp.broadcast_to(m_new, (1, mstate.shape[1]))
        y = hout * lax.rsqrt(jnp.mean(hout * hout, axis=-1, keepdims=True) + EPS)
        y = y * hg_ref[:, h * DV_A:(h + 1) * DV_A]
        o_ref[0, :, h * DV_A:(h + 1) * DV_A] = (
            y * jax.nn.sigmoid(og_ref[0, :, h * DV_A:(h + 1) * DV_A])).astype(o_ref.dtype)


def mlstm_mixer(qk, v, og, gates, conv_w, conv_b, gate_b, hnorm_g, out_dtype=F32):
    bn, L, _ = qk.shape
    T = MLSTM_CHUNK
    nv = H_A * DV_A
    gb = jnp.zeros((1, LANES), F32).at[0, :2 * H_A].set(gate_b.reshape(-1))
    ltri = jnp.asarray(_tri(T))
    row = lambda b, i: (b, i, 0)
    const = lambda b, i: (0, 0)
    return pl.pallas_call(
        _mlstm_kernel,
        grid=(bn, L // T),
        in_specs=[pl.BlockSpec((1, T, qk.shape[-1]), row),
                  pl.BlockSpec((1, T, nv), row),
                  pl.BlockSpec((1, T, nv), row),
                  pl.BlockSpec((1, T, LANES), row),
                  pl.BlockSpec(conv_w.shape, const),
                  pl.BlockSpec((1, conv_b.shape[-1]), const),
                  pl.BlockSpec((1, LANES), const),
                  pl.BlockSpec((1, nv), const),
                  pl.BlockSpec((T, T), const),
                  pl.BlockSpec((T, T), const)],
        out_specs=pl.BlockSpec((1, T, nv), row),
        out_shape=jax.ShapeDtypeStruct((bn, L, nv), out_dtype),
        scratch_shapes=[pltpu.VMEM((8 + T, qk.shape[-1]), F32),
                        pltpu.VMEM((H_A, DK_A, 2 * DV_A), F32),
                        pltpu.VMEM((8, LANES), F32)],
        compiler_params=_cparams("parallel", "arbitrary"),
        name="mlstm_mixer",
    )(qk, v, og, gates, conv_w, conv_b.reshape(1, -1), gb, hnorm_g.reshape(1, -1), ltri, ltri.T)


def _gla_level_consts(T):
    r = np.arange(T)
    mq, mk, bm = [], [], []
    n = T
    while n >= 2:
        start = (r // n) * n
        mid = start + n // 2
        upper = r >= mid
        mq.append((upper[:, None] & (r[None, :] >= mid[:, None]) & (r[None, :] <= r[:, None])).astype(np.float32))
        mk.append((~upper[:, None] & (r[None, :] > r[:, None]) & (r[None, :] < mid[:, None])).astype(np.float32))
        bm.append((upper[:, None] & ~upper[None, :] & (start[:, None] == start[None, :])).astype(np.float32))
        n //= 2
    bm.append(np.eye(T, dtype=np.float32))
    return np.concatenate(mq, 0), np.concatenate(mk, 0), np.stack(bm, 0)


def _gla_kernel(qk_ref, v_ref, gc_ref, rc_ref, wa_ref, ba_ref, hg_ref, ltri_ref, mq_ref, mk_ref, bm_ref,
                o_ref, st):
    T = qk_ref.shape[1]
    nlev = bm_ref.shape[0] - 1
    nq = H_C * DK_C

    @pl.when(pl.program_id(1) == 0)
    def _():
        st[...] = jnp.zeros(st.shape, F32)

    qk = qk_ref[0]
    q = qk[:, :nq] * (DK_C ** -0.5)
    k = qk[:, nq:]
    la = _log_sigmoid(_bdot(gc_ref[0], wa_ref[...]) + ba_ref[...]) * (1.0 / GLA_TAU)
    eq = jnp.exp(_fdot(mq_ref[...], la))
    ek = jnp.exp(_fdot(mk_ref[...], la))
    bfull = _fdot(ltri_ref[...], la)
    bL = bfull[T - 1:T, :]
    qe = q * jnp.exp(bfull)
    kd = k * jnp.exp(bL - bfull)
    eL = jnp.exp(bL)

    for h in range(H_C):
        cs = slice(h * DK_C, (h + 1) * DK_C)
        vs = slice(h * DV_C, (h + 1) * DV_C)
        qh, kh, vh = q[:, cs], k[:, cs], v_ref[0, :, vs]
        A = bm_ref[nlev] * _bdot_nt(qh, kh)
        for lv in range(nlev):
            A = A + bm_ref[lv] * _bdot_nt(qh * eq[lv * T:(lv + 1) * T, cs], kh * ek[lv * T:(lv + 1) * T, cs])
        o = _bdot_nt(qe[:, cs], st[h]) + _bdot(A, vh)
        st[h] = eL[:, cs] * st[h] + _dot_tn(vh, kd[:, cs])
        y = o * lax.rsqrt(jnp.mean(o * o, axis=-1, keepdims=True) + EPS) * hg_ref[:, vs]
        r = rc_ref[0, :, vs]
        o_ref[0, :, vs] = (y * (r * jax.nn.sigmoid(r))).astype(o_ref.dtype)


def gla_mixer(qk, v, gc, rc, w_alpha, b_alpha, hnorm_g, out_dtype=F32):
    bn, L, _ = qk.shape
    T = GLA_CHUNK
    nv = H_C * DV_C
    nq = H_C * DK_C
    wa = jnp.pad(w_alpha, ((0, gc.shape[-1] - w_alpha.shape[0]), (0, 0))).astype(BF16)
    mq, mk, bm = _gla_level_consts(T)
    row = lambda b, i: (b, i, 0)
    const = lambda b, i: (0, 0)
    return pl.pallas_call(
        _gla_kernel,
        grid=(bn, L // T),
        in_specs=[pl.BlockSpec((1, T, 2 * nq), row),
                  pl.BlockSpec((1, T, nv), row),
                  pl.BlockSpec((1, T, gc.shape[-1]), row),
                  pl.BlockSpec((1, T, nv), row),
                  pl.BlockSpec(wa.shape, const),
                  pl.BlockSpec((1, nq), const),
                  pl.BlockSpec((1, nv), const),
                  pl.BlockSpec((T, T), const),
                  pl.BlockSpec(mq.shape, const),
                  pl.BlockSpec(mk.shape, const),
                  pl.BlockSpec(bm.shape, lambda b, i: (0, 0, 0))],
        out_specs=pl.BlockSpec((1, T, nv), row),
        out_shape=jax.ShapeDtypeStruct((bn, L, nv), out_dtype),
        scratch_shapes=[pltpu.VMEM((H_C, DV_C, DK_C), F32)],
        compiler_params=_cparams("parallel", "arbitrary"),
        name="gla_mixer",
    )(qk, v, gc, rc, wa, b_alpha.reshape(1, nq), hnorm_g.reshape(1, nv), jnp.asarray(_tri(T)),
      jnp.asarray(mq), jnp.asarray(mk), jnp.asarray(bm))


def _rel_bucket(dist):
    n = jnp.maximum(dist, 0)
    max_exact = N_BUCKETS // 2
    nf = jnp.maximum(n, 1).astype(F32)
    large = max_exact + (jnp.log(nf / max_exact) / math.log(MAX_DIST / max_exact)
                         * (N_BUCKETS - max_exact)).astype(jnp.int32)
    large = jnp.minimum(large, N_BUCKETS - 1)
    return jnp.where(n < max_exact, n, large)


def _bias_lookup_kernel(bk_ref, rb_ref, o_ref):
    n = bk_ref.shape[0]
    onehot = (lax.broadcasted_iota(jnp.int32, (n, LANES), 1) == bk_ref[...]).astype(F32)
    o_ref[...] = _fdot(onehot, rb_ref[...])


def bias_lookup(rel_bias, dist):
    n = dist.shape[0]
    tn = min(n, 4096)
    bk = jnp.broadcast_to(_rel_bucket(dist)[:, None], (n, LANES))
    rb = jnp.pad(rel_bias, ((0, LANES - rel_bias.shape[0]), (0, LANES - rel_bias.shape[1])))
    return pl.pallas_call(
        _bias_lookup_kernel,
        grid=(n // tn,),
        in_specs=[pl.BlockSpec((tn, LANES), lambda i: (i, 0)), pl.BlockSpec((LANES, LANES), lambda i: (0, 0))],
        out_specs=pl.BlockSpec((tn, LANES), lambda i: (i, 0)),
        out_shape=jax.ShapeDtypeStruct((n, LANES), F32),
        compiler_params=_cparams("parallel"),
        name="bias_lookup",
    )(bk, rb)


def bias_table(rel_bias, L):
    return bias_lookup(rel_bias, jnp.arange(L, dtype=jnp.int32))


DIL_W = 128
DIL_HB = 2


def _dilated_kernel(q_ref, kp_ref, kc_ref, vp_ref, vc_ref, bias_ref, o_ref, kall, vall, num, mx, den):
    W = DIL_W
    tb = q_ref.shape[1]
    first = pl.program_id(2) == 0
    kall[0:tb, :] = kp_ref[0]
    kall[tb:2 * tb, :] = kc_ref[0]
    vall[0:tb, :] = vp_ref[0]
    vall[tb:2 * tb, :] = vc_ref[0]
    num[...] = jnp.zeros(num.shape, F32)
    den[...] = jnp.zeros(den.shape, F32)
    mx[...] = jnp.full(mx.shape, NEG_INF, F32)
    col = lax.broadcasted_iota(jnp.int32, (1, 2 * W), 1)

    for pi, (window, dil) in enumerate(DIL_PATTERNS):
        span = W * dil

        def unit(u, carry, pi=pi, dil=dil, span=span):
            g = u // dil
            q0 = g * span + (u % dil)
            k0 = tb + q0 - span
            qu = q_ref[0, pl.ds(q0, W, stride=dil), :]
            ku = kall[pl.ds(k0, 2 * W, stride=dil), :]
            vu = vall[pl.ds(k0, 2 * W, stride=dil), :]
            edge = jnp.where((col < W) & first & (g == 0), NEG_INF, 0.0).astype(F32)
            for hh in range(DIL_HB):
                cs = slice(hh * DH_D, (hh + 1) * DH_D)
                logits = _bdot_nt(qu[:, cs] * (DH_D ** -0.5), ku[:, cs]) + bias_ref[pi, hh] + edge
                m = jnp.max(logits, axis=1, keepdims=True)
                p = jnp.exp(logits - m)
                nm = _bdot(p, vu[:, cs])
                dn = jnp.sum(p, axis=1, keepdims=True)
                rows = pl.ds(q0, W, stride=dil)
                m0 = mx[hh, rows, :]
                mm = jnp.maximum(m, m0)
                a, b = jnp.exp(m - mm), jnp.exp(m0 - mm)
                num[hh, rows, :] = nm * a + num[hh, rows, :] * b
                den[hh, rows, :] = dn * a + den[hh, rows, :] * b
                mx[hh, rows, :] = mm
            return carry

        lax.fori_loop(0, tb // W, unit, 0, unroll=4)

    for hh in range(DIL_HB):
        o_ref[0, :, hh * DH_D:(hh + 1) * DH_D] = (num[hh] / den[hh]).astype(o_ref.dtype)


def dilated_mixer(q, k, v, rel_bias, out_dtype=F32):
    bn, L, nd = q.shape
    W = DIL_W
    tb = W * max(d for _, d in DIL_PATTERNS)
    cw = DIL_HB * DH_D
    tap = np.arange(W)[:, None] + W - np.arange(2 * W)[None, :]
    ok = jnp.asarray((tap >= 0) & (tap <= W))[None]
    biases = []
    for window, dil in DIL_PATTERNS:
        assert window // dil == W
        tile = bias_lookup(rel_bias, jnp.asarray((np.clip(tap, 0, W) * dil).reshape(-1), jnp.int32))
        biases.append(jnp.where(ok, tile[:, :H_D].reshape(W, 2 * W, H_D).transpose(2, 0, 1), NEG_INF))
    bias = jnp.stack(biases)
    cur = lambda b, c, j: (b, j, c)
    prv = lambda b, c, j: (b, jnp.maximum(j - 1, 0), c)
    blk = (1, tb, cw)
    return pl.pallas_call(
        _dilated_kernel,
        grid=(bn, nd // cw, L // tb),
        in_specs=[pl.BlockSpec(blk, cur), pl.BlockSpec(blk, prv), pl.BlockSpec(blk, cur),
                  pl.BlockSpec(blk, prv), pl.BlockSpec(blk, cur),
                  pl.BlockSpec((len(DIL_PATTERNS), DIL_HB, W, 2 * W), lambda b, c, j: (0, c, 0, 0))],
        out_specs=pl.BlockSpec(blk, cur),
        out_shape=jax.ShapeDtypeStruct((bn, L, nd), out_dtype),
        scratch_shapes=[pltpu.VMEM((2 * tb, cw), F32), pltpu.VMEM((2 * tb, cw), F32),
                        pltpu.VMEM((DIL_HB, tb, DH_D), F32), pltpu.VMEM((DIL_HB, tb, DH_D), F32),
                        pltpu.VMEM((DIL_HB, tb, DH_D), F32)],
        compiler_params=_cparams("parallel", "parallel", "parallel"),
        name="dilated_mixer",
    )(q, k, k, v, v, bias)


INT_MIN = -2 ** 31
DSA_TQ = 256
DSA_KC = 512
DSA_AR = 64


def _split_hi_lo(x):
    hi = x.astype(BF16)
    lo = (x - hi.astype(F32)).astype(BF16)
    return hi, lo


def _dsa_kernel(qb_ref, qi_ref, gt_ref, ki_ref, ckv_ref, wuk_ref, wuv_ref, ftab_ref, o_ref,
                skew, kcat, cke, cket, skey, thr_ref, j_ref, m_ref, acc_ref, madd_ref, lg0_ref, lg1_ref,
                p0_ref, p1_ref, qabs_ref, qcat_ref, alpha_ref, *, topk):
    tq = qb_ref.shape[1]
    L = ki_ref.shape[1]
    kc = DSA_KC
    ar = DSA_AR
    i = pl.program_id(1)
    t0 = i * tq
    nck = (t0 + tq + kc - 1) // kc
    left = lax.broadcasted_iota(jnp.int32, (1, LANES), 1) < (LANES // 2)
    tpos = t0 + lax.broadcasted_iota(jnp.int32, (1, tq), 1)
    krow = lax.broadcasted_iota(jnp.int32, (kc, tq), 0)

    @pl.when((pl.program_id(0) == 0) & (i == 0))
    def _():
        ltot = ftab_ref.shape[1]
        for h in range(H_B):
            x = jnp.broadcast_to(ftab_ref[h:h + 1, :], (LANES, ltot))
            skew[h] = pltpu.roll(x, ltot - LANES, 1, stride=1, stride_axis=0)

    @pl.when(i == 0)
    def _():
        hi, lo = _split_hi_lo(ki_ref[0])
        kcat[:, :LANES] = hi
        kcat[:, LANES:] = jnp.where(left, lo, jnp.zeros_like(lo))
        cke[:, :R_KV] = ckv_ref[0]
        one_col = lax.broadcasted_iota(jnp.int32, (L, LANES), 1) == 0
        cke[:, R_KV:] = jnp.where(one_col, 1.0, 0.0).astype(BF16)
        nr = cket.shape[0]
        eye = (lax.broadcasted_iota(jnp.int32, (nr, R_KV + LANES), 0)
               == lax.broadcasted_iota(jnp.int32, (nr, R_KV + LANES), 1)).astype(BF16)
        for c in range(L // kc):
            cket[:, c * kc:(c + 1) * kc] = _bdot_nt(eye, cke[c * kc:(c + 1) * kc, :]).astype(BF16)

    gt_t = gt_ref[0].T
    ws = []
    for h in range(H_IDX):
        hi, lo = _split_hi_lo(qi_ref[0, :, h * LANES:(h + 1) * LANES])
        qcat_ref[h, :, :LANES] = jnp.where(left, hi, lo)
        qcat_ref[h, :, LANES:] = jnp.where(left, hi, jnp.zeros_like(hi))
        ws.append(gt_t[2 * H_A + h:2 * H_A + h + 1, :] * (H_IDX ** -0.5))

    def score_body(c, carry):
        s0 = pl.multiple_of(c * kc, kc)
        sc = jnp.zeros((kc, tq), F32)
        for h in range(H_IDX):
            sc = sc + ws[h] * jnp.maximum(lax.dot_general(kcat[pl.ds(s0, kc), :], qcat_ref[h],
                                                          (((1,), (1,)), ((), ())),
                                                          preferred_element_type=F32), 0.0)
        bits = lax.bitcast_convert_type(sc + 0.0, jnp.int32)
        key = jnp.where(bits < 0, bits ^ jnp.int32(0x7FFFFFFF), bits)
        skey[pl.ds(s0, kc), :] = jnp.where(s0 + krow <= tpos, key, jnp.int32(INT_MIN))
        return carry

    lax.fori_loop(0, nck, score_body, 0)

    arow = lax.broadcasted_iota(jnp.int32, (ar, tq), 0)

    def count(pred):
        def body(c, acc):
            for a in range(kc // ar):
                r0 = pl.multiple_of(c * kc + a * ar, ar)
                acc = acc + pred(skey[pl.ds(r0, ar), :], r0 + arow).astype(jnp.int32)
            return acc
        acc = lax.fori_loop(0, nck, body, jnp.zeros((ar, tq), jnp.int32))
        return jnp.sum(acc, axis=0, keepdims=True)

    thr_ref[...] = jnp.full(thr_ref.shape, INT_MIN, jnp.int32)
    j_ref[...] = jnp.full(j_ref.shape, L, jnp.int32)

    @pl.when(t0 + tq > topk)
    def _():
        def bit_body(it, tu):
            cand_u = tu | (jnp.int32(1) << (31 - it))
            cand = cand_u ^ jnp.int32(INT_MIN)
            return jnp.where(count(lambda blk, idx: blk >= cand) >= topk, cand_u, tu)

        tu = lax.fori_loop(0, 32, bit_body, jnp.zeros((1, tq), jnp.int32))
        thr = tu ^ jnp.int32(INT_MIN)
        thr_ref[...] = jnp.broadcast_to(thr, thr_ref.shape)
        need = topk - count(lambda blk, idx: blk > thr)
        n_eq = count(lambda blk, idx: blk == thr)
        tied = ((n_eq > need) & (thr != INT_MIN)).astype(jnp.int32)

        @pl.when(jnp.max(tied) > 0)
        def _():
            nbits = max(1, (L - 1).bit_length())

            def jbit_body(it, jv):
                cand = jv | (jnp.int32(1) << (nbits - 1 - it))
                return jnp.where(count(lambda blk, idx: (blk == thr) & (idx < cand)) < need, cand, jv)

            jv = lax.fori_loop(0, nbits, jbit_body, jnp.zeros((1, tq), jnp.int32))
            j_ref[...] = jnp.broadcast_to(jv, j_ref.shape)

    thr = thr_ref[0:1, :]
    jv = j_ref[0:1, :]
    for h in range(H_B):
        qabs_ref[h] = (_bdot_nt(qb_ref[0, :, h * DH_B:(h + 1) * DH_B], wuk_ref[h]) * (DH_B ** -0.5)).astype(BF16)
    m_ref[...] = jnp.full(m_ref.shape, NEG_INF, F32)
    acc_ref[...] = jnp.zeros(acc_ref.shape, F32)

    sb = LANES
    nsb = kc // sb
    srow = lax.broadcasted_iota(jnp.int32, (sb, tq), 0)

    def attn_body(c, carry):
        s0 = pl.multiple_of(c * kc, kc)
        for a in range(nsb):
            blk = skey[pl.ds(s0 + a * sb, sb), :]
            idx = s0 + a * sb + srow
            sel = ((blk > thr) | ((blk == thr) & (idx <= jv))) & (idx <= tpos)
            madd_ref[a * sb:(a + 1) * sb, :] = jnp.where(sel, 0.0, NEG_INF).astype(F32)
        def scores(h, lg):
            lg[...] = _bdot_nt(cke[pl.ds(s0, kc), :R_KV], qabs_ref[h])

        def softmax(h, lg, pb):
            m_old = m_ref[h, 0:1, :]
            m_new = m_old
            for a in range(nsb):
                rows = slice(a * sb, (a + 1) * sb)
                off = pl.multiple_of(t0 - s0 + kc - (a + 1) * sb, sb)
                x = lg[rows, :] + skew[h, :, pl.ds(off, tq)] + madd_ref[rows, :]
                lg[rows, :] = x
                m_new = jnp.maximum(m_new, jnp.max(x, axis=0, keepdims=True))
            for a in range(nsb):
                rows = slice(a * sb, (a + 1) * sb)
                pb[rows, :] = jnp.exp(lg[rows, :] - m_new).astype(BF16)
            alpha_ref[h] = jnp.broadcast_to(jnp.exp(m_old - m_new), alpha_ref.shape[1:])
            m_ref[h] = jnp.broadcast_to(m_new, m_ref.shape[1:])

        def accumulate(h, pb):
            acc_ref[h] = alpha_ref[h, 0:1, :] * acc_ref[h] + jnp.dot(cket[:, pl.ds(s0, kc)], pb[...],
                                                                    preferred_element_type=F32)

        scores(0, lg0_ref)
        scores(1, lg1_ref)
        softmax(0, lg0_ref, p0_ref)
        scores(2, lg0_ref)
        accumulate(0, p0_ref)
        softmax(1, lg1_ref, p1_ref)

        def pair_body(k, carry2):
            e = 2 * k
            scores(e + 1, lg1_ref)
            accumulate(e - 1, p1_ref)
            softmax(e, lg0_ref, p0_ref)
            scores((e + 2) % H_B, lg0_ref)
            accumulate(e, p0_ref)
            softmax(e + 1, lg1_ref, p1_ref)
            return carry2

        lax.fori_loop(1, H_B // 2, pair_body, 0)
        accumulate(H_B - 1, p1_ref)
        return carry

    lax.fori_loop(0, nck, attn_body, 0)
    for h in range(H_B):
        o_lat = acc_ref[h, :R_KV, :] / acc_ref[h, R_KV:R_KV + 1, :]
        o_ref[0, :, h * DH_B:(h + 1) * DH_B] = _dot_tn(o_lat, wuv_ref[h]).astype(o_ref.dtype)


def dsa_mixer(qb, qi, gates, ki, ckv, w_uk, w_uv, table, out_dtype=F32):
    bn, L, _ = qb.shape
    tq = min(DSA_TQ, L)
    topk = min(TOPK_MAX, L // TOPK_DIV)
    wuk = jnp.transpose(w_uk, (1, 0, 2)).astype(BF16)
    wuv = jnp.transpose(w_uv, (1, 0, 2)).astype(BF16)
    ftab = jnp.concatenate([jnp.zeros((8, DSA_KC), F32), table[:, :8].T], axis=1)
    row = lambda b, i: (b, i, 0)
    full = lambda b, i: (b, 0, 0)
    c3 = lambda b, i: (0, 0, 0)
    kern = functools.partial(_dsa_kernel, topk=topk)
    return pl.pallas_call(
        kern,
        grid=(bn, L // tq),
        in_specs=[pl.BlockSpec((1, tq, H_B * DH_B), row),
                  pl.BlockSpec((1, tq, H_IDX * LANES), row),
                  pl.BlockSpec((1, tq, LANES), row),
                  pl.BlockSpec((1, L, LANES), full),
                  pl.BlockSpec((1, L, R_KV), full),
                  pl.BlockSpec(wuk.shape, c3),
                  pl.BlockSpec(wuv.shape, c3),
                  pl.BlockSpec(ftab.shape, lambda b, i: (0, 0))],
        out_specs=pl.BlockSpec((1, tq, H_B * DH_B), row),
        out_shape=jax.ShapeDtypeStruct((bn, L, H_B * DH_B), out_dtype),
        scratch_shapes=[pltpu.VMEM((H_B, LANES, ftab.shape[1]), F32),
                        pltpu.VMEM((L, 2 * LANES), BF16),
                        pltpu.VMEM((L, R_KV + LANES), BF16),
                        pltpu.VMEM((R_KV + 16, L), BF16),
                        pltpu.VMEM((L, tq), jnp.int32),
                        pltpu.VMEM((8, tq), jnp.int32),
                        pltpu.VMEM((8, tq), jnp.int32),
                        pltpu.VMEM((H_B, 8, tq), F32),
                        pltpu.VMEM((H_B, R_KV + 16, tq), F32),
                        pltpu.VMEM((DSA_KC, tq), F32),
                        pltpu.VMEM((DSA_KC, tq), F32),
                        pltpu.VMEM((DSA_KC, tq), F32),
                        pltpu.VMEM((DSA_KC, tq), BF16),
                        pltpu.VMEM((DSA_KC, tq), BF16),
                        pltpu.VMEM((H_B, tq, R_KV), BF16),
                        pltpu.VMEM((H_IDX, tq, 2 * LANES), BF16),
                        pltpu.VMEM((H_B, 8, tq), F32)],
        compiler_params=_cparams("arbitrary", "arbitrary"),
        name="dsa_mixer",
    )(qb, qi, gates, ki, ckv, wuk, wuv, ftab)


def _router_gates(lg):
    lane = lax.broadcasted_iota(jnp.int32, lg.shape, 1)
    ninf = jnp.float32(-jnp.inf)
    lc = jnp.where(lane < N_GROUPS, lg, ninf)
    mc = jnp.max(lc, axis=1, keepdims=True)
    gidx = jnp.min(jnp.where(lc == mc, lane, LANES), axis=1, keepdims=True)
    p_g = 1.0 / jnp.sum(jnp.exp(lc - mc), axis=1, keepdims=True)
    lo = N_GROUPS + E_PER_GROUP * gidx
    lf = jnp.where((lane >= lo) & (lane < lo + E_PER_GROUP), lg, ninf)
    v1 = jnp.max(lf, axis=1, keepdims=True)
    i1 = jnp.min(jnp.where(lf == v1, lane, LANES), axis=1, keepdims=True)
    lf2 = jnp.where(lane == i1, ninf, lf)
    v2 = jnp.max(lf2, axis=1, keepdims=True)
    i2 = jnp.min(jnp.where(lf2 == v2, lane, LANES), axis=1, keepdims=True)
    e2 = jnp.exp(v2 - v1)
    p1 = 1.0 / (1.0 + e2)
    p2 = e2 / (1.0 + e2)
    return p_g * jnp.where(lane == i1, p1, jnp.where(lane == i2, p2, 0.0))


def _moe_dense_kernel(x_ref, g_ref, sc_ref, sh_ref, gt_ref, wr_ref, br_ref, wg_ref, wu_ref, wd_ref, gf_ref,
                      o_ref, hf, gates, acc, *, final):
    e = pl.program_id(2)

    @pl.when(e == 0)
    def _():
        h = _modulated_norm(x_ref[0], g_ref[...], sc_ref[0], sh_ref[0])
        hf[...] = h.astype(BF16)
        gates[...] = _router_gates(_fdot(h, wr_ref[...]) + br_ref[...])
        acc[...] = jnp.zeros(acc.shape, F32)

    lane = lax.broadcasted_iota(jnp.int32, gates.shape, 1)
    gcol = jnp.sum(jnp.where(lane == N_GROUPS + e, gates[...], 0.0), axis=1, keepdims=True)
    hb = hf[...]
    a = jnp.dot(hb, wg_ref[0], preferred_element_type=F32)
    a = a * jax.nn.sigmoid(a) * jnp.dot(hb, wu_ref[0], preferred_element_type=F32) * gcol
    acc[...] += _bdot(a, wd_ref[0])

    @pl.when(e == pl.num_programs(2) - 1)
    def _():
        y = x_ref[0] + gt_ref[0] * acc[...]
        if final:
            y = y * lax.rsqrt(jnp.mean(y * y, axis=-1, keepdims=True) + EPS) * gf_ref[...]
        o_ref[0] = y


def moe_layer(x, g, sc, sh, gt, w_coarse, b_coarse, w_fine, b_fine, w_gate, w_up, w_down, g_final=None, tm=1024):
    bn, L, d = x.shape
    ne = N_GROUPS * E_PER_GROUP
    ff = w_gate.shape[-1]
    wr = jnp.concatenate([w_coarse, jnp.transpose(w_fine, (1, 0, 2)).reshape(d, ne)], axis=1)
    wr = _pad_cols(wr, LANES)
    br = _pad_cols(jnp.concatenate([b_coarse, b_fine.reshape(-1)])[None, :], LANES)
    wg = w_gate.reshape(ne, d, ff).astype(BF16)
    wu = w_up.reshape(ne, d, ff).astype(BF16)
    wd = w_down.reshape(ne, ff, d).astype(BF16)
    final = g_final is not None
    gf = (g_final if final else jnp.ones((d,), F32)).reshape(1, d)
    row = lambda b, i, e: (b, i, 0)
    per_b = lambda b, i, e: (b, 0, 0)
    const = lambda b, i, e: (0, 0)
    per_e = lambda b, i, e: (e, 0, 0)
    kern = functools.partial(_moe_dense_kernel, final=final)
    return pl.pallas_call(
        kern,
        grid=(bn, L // tm, ne),
        in_specs=[pl.BlockSpec((1, tm, d), row),
                  pl.BlockSpec((1, d), const),
                  pl.BlockSpec((1, 1, d), per_b),
                  pl.BlockSpec((1, 1, d), per_b),
                  pl.BlockSpec((1, 1, d), per_b),
                  pl.BlockSpec((d, LANES), const),
                  pl.BlockSpec((1, LANES), const),
                  pl.BlockSpec((1, d, ff), per_e),
                  pl.BlockSpec((1, d, ff), per_e),
                  pl.BlockSpec((1, ff, d), per_e),
                  pl.BlockSpec((1, d), const)],
        out_specs=pl.BlockSpec((1, tm, d), row),
        out_shape=jax.ShapeDtypeStruct((bn, L, d), F32),
        scratch_shapes=[pltpu.VMEM((tm, d), BF16), pltpu.VMEM((tm, LANES), F32), pltpu.VMEM((tm, d), F32)],
        compiler_params=_cparams("parallel", "parallel", "arbitrary"),
        name="moe_layer",
    )(x, g.reshape(1, d), sc.reshape(bn, 1, d), sh.reshape(bn, 1, d), gt.reshape(bn, 1, d), wr, br, wg, wu, wd, gf)


def _pad_cols(w, width):
    return jnp.pad(w, ((0, 0), (0, width - w.shape[1])))


def _segments(widths):
    offs = np.cumsum((0,) + tuple(widths))
    return tuple((int(o), int(w)) for o, w in zip(offs[:-1], widths))


AB_WIDTHS = (2 * H_A * DK_A, H_A * DV_A, H_A * DV_A, LANES, H_B * DH_B, R_KV, H_IDX * LANES, LANES)
AB_DTYPES = (F32, BF16, F32, F32, BF16, BF16, F32, F32)
CD_WIDTHS = (2 * H_C * DK_C, H_C * DV_C, H_C * DV_C, LANES, H_D * DH_D, H_D * DH_D, H_D * DH_D)
CD_DTYPES = (F32, BF16, F32, F32, F32, F32, F32)


def pack_ab_w_in(w):
    d = w.shape[0]
    nqk = 2 * H_A * DK_A
    nv = H_A * DV_A
    o_i = nqk + 2 * nv
    o_qb = o_i + 2 * H_A
    o_ckv = o_qb + H_B * DH_B
    o_qi = o_ckv + R_KV
    o_ki = o_qi + H_IDX * D_IDX
    o_wi = o_ki + D_IDX
    gates = _pad_cols(jnp.concatenate([w[:, o_i:o_qb], w[:, o_wi:o_wi + H_IDX]], axis=1), LANES)
    wqi = w[:, o_qi:o_ki].reshape(d, H_IDX, D_IDX)
    wqi = jnp.concatenate([wqi, wqi], axis=-1).reshape(d, H_IDX * LANES)
    wki = w[:, o_ki:o_wi]
    return jnp.concatenate([w[:, :o_i], gates, w[:, o_qb:o_ckv], w[:, o_ckv:o_qi], wqi, wki, wki],
                           axis=1).astype(BF16)


def pack_cd_w_in(w):
    o_g = 2 * H_C * DK_C + H_C * DV_C
    o_r = o_g + GLA_RANK
    o_q = o_r + H_C * DV_C
    return jnp.concatenate([w[:, :o_g], w[:, o_r:o_q], _pad_cols(w[:, o_g:o_r], LANES), w[:, o_q:]],
                           axis=1).astype(BF16)


def kernel(x, c, w_ada, b_ada, g_mix, g_ffn, g_final, rel_bias, ab_w_in, ab_conv_w, ab_conv_b, ab_gate_b, ab_hnorm_g, ab_w_uk, ab_w_uv, ab_w_out, cd_w_in, cd_w_alpha, cd_b_alpha, cd_hnorm_g, cd_w_out, moe_w_coarse, moe_b_coarse, moe_w_fine, moe_b_fine, moe_w_gate, moe_w_up, moe_w_down):
    depth = w_ada.shape[0]
    L, d = x.shape[1], x.shape[2]
    mod = ada_mod(c, w_ada, b_ada)
    table = bias_table(rel_bias, L)
    for l in range(depth):
        sh_m, sc_m, gt_m, sh_f, sc_f, gt_f = [mod[l, :, i * d:(i + 1) * d] for i in range(6)]
        j = l // 2
        if l % 2 == 0:
            qk, va, oa, gates, qb, ckv, qi, ki = norm_proj(
                x, g_mix[l], sc_m, sh_m, pack_ab_w_in(ab_w_in[j]), _segments(AB_WIDTHS), AB_DTYPES)
            y1 = mlstm_mixer(qk, va, oa, gates, ab_conv_w[j], ab_conv_b[j], ab_gate_b[j], ab_hnorm_g[j],
                             out_dtype=BF16)
            y2 = dsa_mixer(qb, qi, gates, ki, ckv, ab_w_uk[j], ab_w_uv[j], table, out_dtype=BF16)
            w_out = ab_w_out[j]
        else:
            qk, vc, rc, gc, qd, kd, vd = norm_proj(
                x, g_mix[l], sc_m, sh_m, pack_cd_w_in(cd_w_in[j]), _segments(CD_WIDTHS), CD_DTYPES)
            y1 = gla_mixer(qk, vc, gc, rc, cd_w_alpha[j], cd_b_alpha[j], cd_hnorm_g[j], out_dtype=BF16)
            y2 = dilated_mixer(qd, kd, vd, rel_bias, out_dtype=BF16)
            w_out = cd_w_out[j]
        x = outproj_residual(x, y1, y2, gt_m, w_out.astype(BF16))
        x = moe_layer(x, g_ffn[l], sc_f, sh_f, gt_f, moe_w_coarse[l], moe_b_coarse[l], moe_w_fine[l],
                      moe_b_fine[l], moe_w_gate[l], moe_w_up[l], moe_w_down[l],
                      g_final=g_final if l == depth - 1 else None)
    return x
```

```python
import functools
import math

import numpy as np
import jax
import jax.numpy as jnp
from jax import lax
from jax.experimental import pallas as pl
from jax.experimental.pallas import tpu as pltpu

F32 = jnp.float32
BF16 = jnp.bfloat16

EPS = 1e-6
NEG_INF = -1e30

H_A, DK_A, DV_A, CONV_A, MLSTM_CHUNK = 4, 64, 128, 4, 64
H_B, DH_B, R_KV, H_IDX, D_IDX = 8, 64, 128, 4, 64
TOPK_MAX, TOPK_DIV = 256, 4
H_C, DK_C, DV_C, GLA_RANK, GLA_TAU, GLA_CHUNK = 4, 64, 128, 16, 16.0, 64
H_D, DH_D = 8, 64
DIL_PATTERNS = ((128, 1), (512, 4), (2048, 16))
N_BUCKETS, MAX_DIST = 32, 2048
N_GROUPS, E_PER_GROUP, EXPERT_FF = 4, 4, 512

LANES = 128
VMEM_LIMIT = 56 * 1024 * 1024


def _cparams(*sem):
    return pltpu.CompilerParams(dimension_semantics=sem, vmem_limit_bytes=VMEM_LIMIT)


def _bdot(a, b):
    return jnp.dot(a.astype(BF16), b.astype(BF16), preferred_element_type=F32)


def _bdot_nt(a, b):
    return lax.dot_general(a.astype(BF16), b.astype(BF16), (((1,), (1,)), ((), ())),
                           preferred_element_type=F32)


def _fdot(a, b):
    return jnp.dot(a, b, preferred_element_type=F32, precision=lax.Precision.HIGHEST)


def _ada_kernel(c_ref, w_ref, b_ref, o_ref):
    c = c_ref[...]
    a = c * jax.nn.sigmoid(c)
    o_ref[0] = _fdot(a, w_ref[0]) + b_ref[0]


def ada_mod(c, w_ada, b_ada):
    depth, d, e = w_ada.shape
    bn = c.shape[0]
    tn = 1536
    return pl.pallas_call(
        _ada_kernel,
        grid=(depth, e // tn),
        in_specs=[pl.BlockSpec((bn, d), lambda l, j: (0, 0)),
                  pl.BlockSpec((1, d, tn), lambda l, j: (l, 0, j)),
                  pl.BlockSpec((1, 1, tn), lambda l, j: (l, 0, j))],
        out_specs=pl.BlockSpec((1, bn, tn), lambda l, j: (l, 0, j)),
        out_shape=jax.ShapeDtypeStruct((depth, bn, e), F32),
        compiler_params=_cparams("arbitrary", "arbitrary"),
        name="ada_mod",
    )(c, w_ada, b_ada.reshape(depth, 1, e))


def _modulated_norm(x, g, sc, sh):
    y = x * lax.rsqrt(jnp.mean(x * x, axis=-1, keepdims=True) + EPS) * g
    return y * (1.0 + sc) + sh


def _norm_proj_kernel(x_ref, g_ref, sc_ref, sh_ref, w_ref, *o_refs, segs):
    h = _modulated_norm(x_ref[0], g_ref[...], sc_ref[0], sh_ref[0]).astype(BF16)
    for o_ref, (start, width) in zip(o_refs, segs):
        o_ref[0] = jnp.dot(h, w_ref[:, start:start + width], preferred_element_type=F32).astype(o_ref.dtype)


def norm_proj(x, g, sc, sh, w, segs, dtypes, tm=512):
    bn, L, d = x.shape
    wp = w.shape[1]
    kern = functools.partial(_norm_proj_kernel, segs=tuple(segs))
    return pl.pallas_call(
        kern,
        grid=(bn, L // tm),
        in_specs=[pl.BlockSpec((1, tm, d), lambda b, i: (b, i, 0)),
                  pl.BlockSpec((1, d), lambda b, i: (0, 0)),
                  pl.BlockSpec((1, 1, d), lambda b, i: (b, 0, 0)),
                  pl.BlockSpec((1, 1, d), lambda b, i: (b, 0, 0)),
                  pl.BlockSpec((d, wp), lambda b, i: (0, 0))],
        out_specs=[pl.BlockSpec((1, tm, wd), lambda b, i: (b, i, 0)) for (_, wd) in segs],
        out_shape=[jax.ShapeDtypeStruct((bn, L, wd), dt) for (_, wd), dt in zip(segs, dtypes)],
        compiler_params=_cparams("parallel", "parallel"),
        name="norm_proj",
    )(x, g.reshape(1, d), sc.reshape(bn, 1, d), sh.reshape(bn, 1, d), w)


def _outproj_kernel(x_ref, ya_ref, yb_ref, gt_ref, w_ref, o_ref):
    na = ya_ref.shape[-1]
    y = _bdot(ya_ref[0], w_ref[:na, :]) + _bdot(yb_ref[0], w_ref[na:, :])
    o_ref[0] = x_ref[0] + gt_ref[0] * y


def outproj_residual(x, ya, yb, gt, w, tm=512):
    bn, L, d = x.shape
    na, nb = ya.shape[-1], yb.shape[-1]
    return pl.pallas_call(
        _outproj_kernel,
        grid=(bn, L // tm),
        in_specs=[pl.BlockSpec((1, tm, d), lambda b, i: (b, i, 0)),
                  pl.BlockSpec((1, tm, na), lambda b, i: (b, i, 0)),
                  pl.BlockSpec((1, tm, nb), lambda b, i: (b, i, 0)),
                  pl.BlockSpec((1, 1, d), lambda b, i: (b, 0, 0)),
                  pl.BlockSpec((na + nb, d), lambda b, i: (0, 0))],
        out_specs=pl.BlockSpec((1, tm, d), lambda b, i: (b, i, 0)),
        out_shape=jax.ShapeDtypeStruct((bn, L, d), F32),
        compiler_params=_cparams("parallel", "parallel"),
        name="outproj_residual",
    )(x, ya, yb, gt.reshape(bn, 1, d), w)


def _log_sigmoid(x):
    return jnp.minimum(x, 0.0) - jnp.log1p(jnp.exp(-jnp.abs(x)))


def _dot_tn(a, b):
    return lax.dot_general(a.astype(BF16), b.astype(BF16), (((0,), (0,)), ((), ())),
                           preferred_element_type=F32)


def _tri(T):
    r = np.arange(T)
    return (r[:, None] >= r[None, :]).astype(np.float32)


SCAN_NB = 2


def _mlstm_kernel(qk_ref, v_ref, og_ref, gt_ref, cw_ref, cb_ref, gb_ref, hg_ref, ltri_ref, utri_ref,
                  o_ref, xbuf, cext, mstate):
    T = qk_ref.shape[1]
    tail = xbuf.shape[1] - T

    @pl.when(pl.program_id(1) == 0)
    def _():
        xbuf[:, 0:tail, :] = jnp.zeros((xbuf.shape[0], tail, xbuf.shape[2]), F32)
        cext[...] = jnp.zeros(cext.shape, F32)
        mstate[...] = jnp.full(mstate.shape, NEG_INF, F32)

    for bi in range(qk_ref.shape[0]):
        sl = pl.ds(bi, 1)
        _mlstm_chunk(qk_ref.at[sl], v_ref.at[sl], og_ref.at[sl], gt_ref.at[sl], cw_ref, cb_ref, gb_ref, hg_ref,
                     ltri_ref, utri_ref, o_ref.at[sl], xbuf.at[bi], cext.at[bi], mstate.at[bi])


def _mlstm_chunk(qk_ref, v_ref, og_ref, gt_ref, cw_ref, cb_ref, gb_ref, hg_ref, ltri_ref, utri_ref,
                 o_ref, xbuf, cext, mstate):
    T = qk_ref.shape[1]
    tail = xbuf.shape[0] - T
    x = qk_ref[0]
    xbuf[tail:tail + T, :] = x
    conv = cb_ref[...] + sum(xbuf[pl.ds(tail - (CONV_A - 1) + j, T), :] * cw_ref[j:j + 1, :]
                             for j in range(CONV_A))
    xbuf[0:tail, :] = x[T - tail:T, :]
    qk = conv * jax.nn.sigmoid(conv)

    G = gt_ref[0] + gb_ref[...]
    GT = G.T
    bcol = _fdot(ltri_ref[...], _log_sigmoid(G))
    brow = _fdot(_log_sigmoid(GT), utri_ref[...])
    causal = ltri_ref[...] > 0.5
    lane = lax.broadcasted_iota(jnp.int32, (T, DV_A), 1)
    ones_blk = jnp.where(lane == 0, 1.0, 0.0).astype(F32)
    nq = H_A * DK_A

    for h in range(H_A):
        b_c = bcol[:, H_A + h:H_A + h + 1]
        b_r = brow[H_A + h:H_A + h + 1, :]
        li_r = GT[h:h + 1, :]
        li_c = G[:, h:h + 1]
        m_prev = mstate[h:h + 1, 0:1]
        Dm = jnp.where(causal, b_c - b_r + li_r, NEG_INF)
        g = b_c + m_prev
        mt = jnp.maximum(g, jnp.max(Dm, axis=1, keepdims=True))
        w_intra = jnp.exp(Dm - mt)
        w_state = jnp.exp(g - mt)
        q = qk[:, h * DK_A:(h + 1) * DK_A] * (DK_A ** -0.5)
        k = qk[:, nq + h * DK_A:nq + (h + 1) * DK_A]
        vext = jnp.concatenate([v_ref[0, :, h * DV_A:(h + 1) * DV_A].astype(F32), ones_blk], axis=1)
        s = _bdot_nt(q, k) * w_intra
        nd = w_state * _bdot(q, cext[h]) + _bdot(s, vext)
        num = nd[:, :DV_A]
        den = nd[:, DV_A:DV_A + 1]
        hout = num / jnp.maximum(jnp.abs(den), jnp.exp(-mt))
        bL = b_c[T - 1:T, :]
        d_end = bL - b_c + li_c
        m_new = jnp.maximum(bL + m_prev, jnp.max(d_end, axis=0, keepdims=True))
        w_e = jnp.exp(d_end - m_new)
        decay = jnp.exp(bL + m_prev - m_new)
        cext[h] = decay * cext[h] + _dot_tn(k * w_e, vext)
        mstate[h:h + 1, :] = jnp.broadcast_to(m_new, (1, mstate.shape[1]))
        y = hout * lax.rsqrt(jnp.mean(hout * hout, axis=-1, keepdims=True) + EPS)
        y = y * hg_ref[:, h * DV_A:(h + 1) * DV_A]
        o_ref[0, :, h * DV_A:(h + 1) * DV_A] = (
            y * jax.nn.sigmoid(og_ref[0, :, h * DV_A:(h + 1) * DV_A])).astype(o_ref.dtype)


def mlstm_mixer(qk, v, og, gates, conv_w, conv_b, gate_b, hnorm_g, out_dtype=F32):
    bn, L, _ = qk.shape
    T = MLSTM_CHUNK
    nv = H_A * DV_A
    gb = jnp.zeros((1, LANES), F32).at[0, :2 * H_A].set(gate_b.reshape(-1))
    ltri = jnp.asarray(_tri(T))
    row = lambda b, i: (b, i, 0)
    const = lambda b, i: (0, 0)
    nb = SCAN_NB if bn % SCAN_NB == 0 else 1
    return pl.pallas_call(
        _mlstm_kernel,
        grid=(bn // nb, L // T),
        in_specs=[pl.BlockSpec((nb, T, qk.shape[-1]), row),
                  pl.BlockSpec((nb, T, nv), row),
                  pl.BlockSpec((nb, T, nv), row),
                  pl.BlockSpec((nb, T, LANES), row),
                  pl.BlockSpec(conv_w.shape, const),
                  pl.BlockSpec((1, conv_b.shape[-1]), const),
                  pl.BlockSpec((1, LANES), const),
                  pl.BlockSpec((1, nv), const),
                  pl.BlockSpec((T, T), const),
                  pl.BlockSpec((T, T), const)],
        out_specs=pl.BlockSpec((nb, T, nv), row),
        out_shape=jax.ShapeDtypeStruct((bn, L, nv), out_dtype),
        scratch_shapes=[pltpu.VMEM((nb, 8 + T, qk.shape[-1]), F32),
                        pltpu.VMEM((nb, H_A, DK_A, 2 * DV_A), F32),
                        pltpu.VMEM((nb, 8, LANES), F32)],
        compiler_params=_cparams("parallel", "arbitrary"),
        name="mlstm_mixer",
    )(qk, v, og, gates, conv_w, conv_b.reshape(1, -1), gb, hnorm_g.reshape(1, -1), ltri, ltri.T)


def _gla_level_consts(T):
    r = np.arange(T)
    mq, mk, bm = [], [], []
    n = T
    while n >= 2:
        start = (r // n) * n
        mid = start + n // 2
        upper = r >= mid
        mq.append((upper[:, None] & (r[None, :] >= mid[:, None]) & (r[None, :] <= r[:, None])).astype(np.float32))
        mk.append((~upper[:, None] & (r[None, :] > r[:, None]) & (r[None, :] < mid[:, None])).astype(np.float32))
        bm.append((upper[:, None] & ~upper[None, :] & (start[:, None] == start[None, :])).astype(np.float32))
        n //= 2
    bm.append(np.eye(T, dtype=np.float32))
    return np.concatenate(mq, 0), np.concatenate(mk, 0), np.stack(bm, 0)


def _gla_kernel(qk_ref, v_ref, gc_ref, rc_ref, wa_ref, ba_ref, hg_ref, ltri_ref, mq_ref, mk_ref, bm_ref,
                o_ref, st):
    @pl.when(pl.program_id(1) == 0)
    def _():
        st[...] = jnp.zeros(st.shape, F32)

    for bi in range(qk_ref.shape[0]):
        sl = pl.ds(bi, 1)
        _gla_chunk(qk_ref.at[sl], v_ref.at[sl], gc_ref.at[sl], rc_ref.at[sl], wa_ref, ba_ref, hg_ref, ltri_ref,
                   mq_ref, mk_ref, bm_ref, o_ref.at[sl], st.at[bi])


def _gla_chunk(qk_ref, v_ref, gc_ref, rc_ref, wa_ref, ba_ref, hg_ref, ltri_ref, mq_ref, mk_ref, bm_ref,
               o_ref, st):
    T = qk_ref.shape[1]
    nlev = bm_ref.shape[0] - 1
    nq = H_C * DK_C

    qk = qk_ref[0]
    q = qk[:, :nq] * (DK_C ** -0.5)
    k = qk[:, nq:]
    la = _log_sigmoid(_bdot(gc_ref[0], wa_ref[...]) + ba_ref[...]) * (1.0 / GLA_TAU)
    eq = jnp.exp(_fdot(mq_ref[...], la))
    ek = jnp.exp(_fdot(mk_ref[...], la))
    bfull = _fdot(ltri_ref[...], la)
    bL = bfull[T - 1:T, :]
    qe = q * jnp.exp(bfull)
    kd = k * jnp.exp(bL - bfull)
    eL = jnp.exp(bL)

    for h in range(H_C):
        cs = slice(h * DK_C, (h + 1) * DK_C)
        vs = slice(h * DV_C, (h + 1) * DV_C)
        qh, kh, vh = q[:, cs], k[:, cs], v_ref[0, :, vs]
        A = bm_ref[nlev] * _bdot_nt(qh, kh)
        for lv in range(nlev):
            A = A + bm_ref[lv] * _bdot_nt(qh * eq[lv * T:(lv + 1) * T, cs], kh * ek[lv * T:(lv + 1) * T, cs])
        o = _bdot_nt(qe[:, cs], st[h]) + _bdot(A, vh)
        st[h] = eL[:, cs] * st[h] + _dot_tn(vh, kd[:, cs])
        y = o * lax.rsqrt(jnp.mean(o * o, axis=-1, keepdims=True) + EPS) * hg_ref[:, vs]
        r = rc_ref[0, :, vs]
        o_ref[0, :, vs] = (y * (r * jax.nn.sigmoid(r))).astype(o_ref.dtype)


def gla_mixer(qk, v, gc, rc, w_alpha, b_alpha, hnorm_g, out_dtype=F32):
    bn, L, _ = qk.shape
    T = GLA_CHUNK
    nv = H_C * DV_C
    nq = H_C * DK_C
    wa = jnp.pad(w_alpha, ((0, gc.shape[-1] - w_alpha.shape[0]), (0, 0))).astype(BF16)
    mq, mk, bm = _gla_level_consts(T)
    row = lambda b, i: (b, i, 0)
    const = lambda b, i: (0, 0)
    nb = SCAN_NB if bn % SCAN_NB == 0 else 1
    return pl.pallas_call(
        _gla_kernel,
        grid=(bn // nb, L // T),
        in_specs=[pl.BlockSpec((nb, T, 2 * nq), row),
                  pl.BlockSpec((nb, T, nv), row),
                  pl.BlockSpec((nb, T, gc.shape[-1]), row),
                  pl.BlockSpec((nb, T, nv), row),
                  pl.BlockSpec(wa.shape, const),
                  pl.BlockSpec((1, nq), const),
                  pl.BlockSpec((1, nv), const),
                  pl.BlockSpec((T, T), const),
                  pl.BlockSpec(mq.shape, const),
                  pl.BlockSpec(mk.shape, const),
                  pl.BlockSpec(bm.shape, lambda b, i: (0, 0, 0))],
        out_specs=pl.BlockSpec((nb, T, nv), row),
        out_shape=jax.ShapeDtypeStruct((bn, L, nv), out_dtype),
        scratch_shapes=[pltpu.VMEM((nb, H_C, DV_C, DK_C), F32)],
        compiler_params=_cparams("parallel", "arbitrary"),
        name="gla_mixer",
    )(qk, v, gc, rc, wa, b_alpha.reshape(1, nq), hnorm_g.reshape(1, nv), jnp.asarray(_tri(T)),
      jnp.asarray(mq), jnp.asarray(mk), jnp.asarray(bm))


def _rel_bucket(dist):
    n = jnp.maximum(dist, 0)
    max_exact = N_BUCKETS // 2
    nf = jnp.maximum(n, 1).astype(F32)
    large = max_exact + (jnp.log(nf / max_exact) / math.log(MAX_DIST / max_exact)
                         * (N_BUCKETS - max_exact)).astype(jnp.int32)
    large = jnp.minimum(large, N_BUCKETS - 1)
    return jnp.where(n < max_exact, n, large)


def _bias_lookup_kernel(bk_ref, rb_ref, o_ref):
    n = bk_ref.shape[0]
    onehot = (lax.broadcasted_iota(jnp.int32, (n, LANES), 1) == bk_ref[...]).astype(F32)
    o_ref[...] = _fdot(onehot, rb_ref[...])


def bias_lookup(rel_bias, dist):
    n = dist.shape[0]
    tn = min(n, 4096)
    bk = jnp.broadcast_to(_rel_bucket(dist)[:, None], (n, LANES))
    rb = jnp.pad(rel_bias, ((0, LANES - rel_bias.shape[0]), (0, LANES - rel_bias.shape[1])))
    return pl.pallas_call(
        _bias_lookup_kernel,
        grid=(n // tn,),
        in_specs=[pl.BlockSpec((tn, LANES), lambda i: (i, 0)), pl.BlockSpec((LANES, LANES), lambda i: (0, 0))],
        out_specs=pl.BlockSpec((tn, LANES), lambda i: (i, 0)),
        out_shape=jax.ShapeDtypeStruct((n, LANES), F32),
        compiler_params=_cparams("parallel"),
        name="bias_lookup",
    )(bk, rb)


def bias_table(rel_bias, L):
    return bias_lookup(rel_bias, jnp.arange(L, dtype=jnp.int32))


DIL_W = 128
DIL_HB = 2


def _dilated_kernel(q_ref, kp_ref, kc_ref, vp_ref, vc_ref, bias_ref, o_ref, kall, vall, num, mx, den):
    W = DIL_W
    tb = q_ref.shape[1]
    first = pl.program_id(2) == 0
    kall[0:tb, :] = kp_ref[0]
    kall[tb:2 * tb, :] = kc_ref[0]
    vall[0:tb, :] = vp_ref[0]
    vall[tb:2 * tb, :] = vc_ref[0]
    num[...] = jnp.zeros(num.shape, F32)
    den[...] = jnp.zeros(den.shape, F32)
    mx[...] = jnp.full(mx.shape, NEG_INF, F32)
    col = lax.broadcasted_iota(jnp.int32, (1, 2 * W), 1)

    for pi, (window, dil) in enumerate(DIL_PATTERNS):
        span = W * dil

        def unit(u, carry, pi=pi, dil=dil, span=span):
            g = u // dil
            q0 = g * span + (u % dil)
            k0 = tb + q0 - span
            qu = q_ref[0, pl.ds(q0, W, stride=dil), :]
            ku = kall[pl.ds(k0, 2 * W, stride=dil), :]
            vu = vall[pl.ds(k0, 2 * W, stride=dil), :]
            edge = jnp.where((col < W) & first & (g == 0), NEG_INF, 0.0).astype(F32)
            for hh in range(DIL_HB):
                cs = slice(hh * DH_D, (hh + 1) * DH_D)
                logits = _bdot_nt(qu[:, cs] * (DH_D ** -0.5), ku[:, cs]) + bias_ref[pi, hh] + edge
                m = jnp.max(logits, axis=1, keepdims=True)
                p = jnp.exp(logits - m)
                nm = _bdot(p, vu[:, cs])
                dn = jnp.sum(p, axis=1, keepdims=True)
                rows = pl.ds(q0, W, stride=dil)
                m0 = mx[hh, rows, :]
                mm = jnp.maximum(m, m0)
                a, b = jnp.exp(m - mm), jnp.exp(m0 - mm)
                num[hh, rows, :] = nm * a + num[hh, rows, :] * b
                den[hh, rows, :] = dn * a + den[hh, rows, :] * b
                mx[hh, rows, :] = mm
            return carry

        lax.fori_loop(0, tb // W, unit, 0, unroll=4)

    for hh in range(DIL_HB):
        o_ref[0, :, hh * DH_D:(hh + 1) * DH_D] = (num[hh] / den[hh]).astype(o_ref.dtype)


def dilated_mixer(q, k, v, rel_bias, out_dtype=F32):
    bn, L, nd = q.shape
    W = DIL_W
    tb = W * max(d for _, d in DIL_PATTERNS)
    cw = DIL_HB * DH_D
    tap = np.arange(W)[:, None] + W - np.arange(2 * W)[None, :]
    ok = jnp.asarray((tap >= 0) & (tap <= W))[None]
    biases = []
    for window, dil in DIL_PATTERNS:
        assert window // dil == W
        tile = bias_lookup(rel_bias, jnp.asarray((np.clip(tap, 0, W) * dil).reshape(-1), jnp.int32))
        biases.append(jnp.where(ok, tile[:, :H_D].reshape(W, 2 * W, H_D).transpose(2, 0, 1), NEG_INF))
    bias = jnp.stack(biases)
    cur = lambda b, c, j: (b, j, c)
    prv = lambda b, c, j: (b, jnp.maximum(j - 1, 0), c)
    blk = (1, tb, cw)
    return pl.pallas_call(
        _dilated_kernel,
        grid=(bn, nd // cw, L // tb),
        in_specs=[pl.BlockSpec(blk, cur), pl.BlockSpec(blk, prv), pl.BlockSpec(blk, cur),
                  pl.BlockSpec(blk, prv), pl.BlockSpec(blk, cur),
                  pl.BlockSpec((len(DIL_PATTERNS), DIL_HB, W, 2 * W), lambda b, c, j: (0, c, 0, 0))],
        out_specs=pl.BlockSpec(blk, cur),
        out_shape=jax.ShapeDtypeStruct((bn, L, nd), out_dtype),
        scratch_shapes=[pltpu.VMEM((2 * tb, cw), F32), pltpu.VMEM((2 * tb, cw), F32),
                        pltpu.VMEM((DIL_HB, tb, DH_D), F32), pltpu.VMEM((DIL_HB, tb, DH_D), F32),
                        pltpu.VMEM((DIL_HB, tb, DH_D), F32)],
        compiler_params=_cparams("parallel", "parallel", "parallel"),
        name="dilated_mixer",
    )(q, k, k, v, v, bias)


INT_MIN = -2 ** 31
DSA_TQ = 256
DSA_KC = 512
DSA_AR = 64


def _split_hi_lo(x):
    hi = x.astype(BF16)
    lo = (x - hi.astype(F32)).astype(BF16)
    return hi, lo


def _dsa_kernel(qb_ref, qi_ref, gt_ref, ki_ref, ckv_ref, wuk_ref, wuv_ref, ftab_ref, o_ref,
                skew, kcat, cke, cket, skey, thr_ref, j_ref, m_ref, acc_ref, madd_ref, lg0_ref, lg1_ref,
                p0_ref, p1_ref, qabs_ref, qcat_ref, alpha_ref, *, topk):
    tq = qb_ref.shape[1]
    L = ki_ref.shape[1]
    kc = DSA_KC
    ar = DSA_AR
    i = pl.program_id(1)
    t0 = i * tq
    nck = (t0 + tq + kc - 1) // kc
    left = lax.broadcasted_iota(jnp.int32, (1, LANES), 1) < (LANES // 2)
    tpos = t0 + lax.broadcasted_iota(jnp.int32, (1, tq), 1)
    krow = lax.broadcasted_iota(jnp.int32, (kc, tq), 0)

    @pl.when((pl.program_id(0) == 0) & (i == 0))
    def _():
        ltot = ftab_ref.shape[1]
        for h in range(H_B):
            x = jnp.broadcast_to(ftab_ref[h:h + 1, :], (LANES, ltot))
            skew[h] = pltpu.roll(x, ltot - LANES, 1, stride=1, stride_axis=0)

    @pl.when(i == 0)
    def _():
        hi, lo = _split_hi_lo(ki_ref[0])
        kcat[:, :LANES] = hi
        kcat[:, LANES:] = jnp.where(left, lo, jnp.zeros_like(lo))
        cke[:, :R_KV] = ckv_ref[0]
        one_col = lax.broadcasted_iota(jnp.int32, (L, LANES), 1) == 0
        cke[:, R_KV:] = jnp.where(one_col, 1.0, 0.0).astype(BF16)
        nr = cket.shape[0]
        eye = (lax.broadcasted_iota(jnp.int32, (nr, R_KV + LANES), 0)
               == lax.broadcasted_iota(jnp.int32, (nr, R_KV + LANES), 1)).astype(BF16)
        for c in range(L // kc):
            cket[:, c * kc:(c + 1) * kc] = _bdot_nt(eye, cke[c * kc:(c + 1) * kc, :]).astype(BF16)

    gt_t = gt_ref[0].T
    ws = []
    for h in range(H_IDX):
        hi, lo = _split_hi_lo(qi_ref[0, :, h * LANES:(h + 1) * LANES])
        qcat_ref[h, :, :LANES] = jnp.where(left, hi, lo)
        qcat_ref[h, :, LANES:] = jnp.where(left, hi, jnp.zeros_like(hi))
        ws.append(gt_t[2 * H_A + h:2 * H_A + h + 1, :] * (H_IDX ** -0.5))

    def score_body(c, carry):
        s0 = pl.multiple_of(c * kc, kc)
        sc = jnp.zeros((kc, tq), F32)
        for h in range(H_IDX):
            sc = sc + ws[h] * jnp.maximum(lax.dot_general(kcat[pl.ds(s0, kc), :], qcat_ref[h],
                                                          (((1,), (1,)), ((), ())),
                                                          preferred_element_type=F32), 0.0)
        bits = lax.bitcast_convert_type(sc + 0.0, jnp.int32)
        key = jnp.where(bits < 0, bits ^ jnp.int32(0x7FFFFFFF), bits)
        skey[pl.ds(s0, kc), :] = jnp.where(s0 + krow <= tpos, key, jnp.int32(INT_MIN))
        return carry

    lax.fori_loop(0, nck, score_body, 0)

    arow = lax.broadcasted_iota(jnp.int32, (ar, tq), 0)

    def count(pred):
        def body(c, acc):
            for a in range(kc // ar):
                r0 = pl.multiple_of(c * kc + a * ar, ar)
                acc = acc + pred(skey[pl.ds(r0, ar), :], r0 + arow).astype(jnp.int32)
            return acc
        acc = lax.fori_loop(0, nck, body, jnp.zeros((ar, tq), jnp.int32))
        return jnp.sum(acc, axis=0, keepdims=True)

    thr_ref[...] = jnp.full(thr_ref.shape, INT_MIN, jnp.int32)
    j_ref[...] = jnp.full(j_ref.shape, L, jnp.int32)

    @pl.when(t0 + tq > topk)
    def _():
        def bit_body(it, tu):
            cand_u = tu | (jnp.int32(1) << (31 - it))
            cand = cand_u ^ jnp.int32(INT_MIN)
            return jnp.where(count(lambda blk, idx: blk >= cand) >= topk, cand_u, tu)

        tu = lax.fori_loop(0, 32, bit_body, jnp.zeros((1, tq), jnp.int32))
        thr = tu ^ jnp.int32(INT_MIN)
        thr_ref[...] = jnp.broadcast_to(thr, thr_ref.shape)
        need = topk - count(lambda blk, idx: blk > thr)
        n_eq = count(lambda blk, idx: blk == thr)
        tied = ((n_eq > need) & (thr != INT_MIN)).astype(jnp.int32)

        @pl.when(jnp.max(tied) > 0)
        def _():
            nbits = max(1, (L - 1).bit_length())

            def jbit_body(it, jv):
                cand = jv | (jnp.int32(1) << (nbits - 1 - it))
                return jnp.where(count(lambda blk, idx: (blk == thr) & (idx < cand)) < need, cand, jv)

            jv = lax.fori_loop(0, nbits, jbit_body, jnp.zeros((1, tq), jnp.int32))
            j_ref[...] = jnp.broadcast_to(jv, j_ref.shape)

    thr = thr_ref[0:1, :]
    jv = j_ref[0:1, :]
    for h in range(H_B):
        qabs_ref[h] = (_bdot_nt(qb_ref[0, :, h * DH_B:(h + 1) * DH_B], wuk_ref[h]) * (DH_B ** -0.5)).astype(BF16)
    m_ref[...] = jnp.full(m_ref.shape, NEG_INF, F32)
    acc_ref[...] = jnp.zeros(acc_ref.shape, F32)

    sb = LANES
    nsb = kc // sb
    srow = lax.broadcasted_iota(jnp.int32, (sb, tq), 0)

    def attn_body(c, carry):
        s0 = pl.multiple_of(c * kc, kc)
        for a in range(nsb):
            blk = skey[pl.ds(s0 + a * sb, sb), :]
            idx = s0 + a * sb + srow
            sel = ((blk > thr) | ((blk == thr) & (idx <= jv))) & (idx <= tpos)
            madd_ref[a * sb:(a + 1) * sb, :] = jnp.where(sel, 0.0, NEG_INF).astype(F32)
        def scores(h, lg):
            lg[...] = _bdot_nt(cke[pl.ds(s0, kc), :R_KV], qabs_ref[h])

        def softmax(h, lg, pb):
            m_old = m_ref[h, 0:1, :]
            m_new = m_old
            for a in range(nsb):
                rows = slice(a * sb, (a + 1) * sb)
                off = pl.multiple_of(t0 - s0 + kc - (a + 1) * sb, sb)
                x = lg[rows, :] + skew[h, :, pl.ds(off, tq)] + madd_ref[rows, :]
                lg[rows, :] = x
                m_new = jnp.maximum(m_new, jnp.max(x, axis=0, keepdims=True))
            for a in range(nsb):
                rows = slice(a * sb, (a + 1) * sb)
                pb[rows, :] = jnp.exp(lg[rows, :] - m_new).astype(BF16)
            alpha_ref[h] = jnp.broadcast_to(jnp.exp(m_old - m_new), alpha_ref.shape[1:])
            m_ref[h] = jnp.broadcast_to(m_new, m_ref.shape[1:])

        def accumulate(h, pb):
            acc_ref[h] = alpha_ref[h, 0:1, :] * acc_ref[h] + jnp.dot(cket[:, pl.ds(s0, kc)], pb[...],
                                                                    preferred_element_type=F32)

        scores(0, lg0_ref)
        scores(1, lg1_ref)
        softmax(0, lg0_ref, p0_ref)
        scores(2, lg0_ref)
        accumulate(0, p0_ref)
        softmax(1, lg1_ref, p1_ref)

        def pair_body(k, carry2):
            e = 2 * k
            scores(e + 1, lg1_ref)
            accumulate(e - 1, p1_ref)
            softmax(e, lg0_ref, p0_ref)
            scores((e + 2) % H_B, lg0_ref)
            accumulate(e, p0_ref)
            softmax(e + 1, lg1_ref, p1_ref)
            return carry2

        lax.fori_loop(1, H_B // 2, pair_body, 0)
        accumulate(H_B - 1, p1_ref)
        return carry

    lax.fori_loop(0, nck, attn_body, 0)
    for h in range(H_B):
        o_lat = acc_ref[h, :R_KV, :] / acc_ref[h, R_KV:R_KV + 1, :]
        o_ref[0, :, h * DH_B:(h + 1) * DH_B] = _dot_tn(o_lat, wuv_ref[h]).astype(o_ref.dtype)


def dsa_mixer(qb, qi, gates, ki, ckv, w_uk, w_uv, table, out_dtype=F32):
    bn, L, _ = qb.shape
    tq = min(DSA_TQ, L)
    topk = min(TOPK_MAX, L // TOPK_DIV)
    wuk = jnp.transpose(w_uk, (1, 0, 2)).astype(BF16)
    wuv = jnp.transpose(w_uv, (1, 0, 2)).astype(BF16)
    ftab = jnp.concatenate([jnp.zeros((8, DSA_KC), F32), table[:, :8].T], axis=1)
    row = lambda b, i: (b, i, 0)
    full = lambda b, i: (b, 0, 0)
    c3 = lambda b, i: (0, 0, 0)
    kern = functools.partial(_dsa_kernel, topk=topk)
    return pl.pallas_call(
        kern,
        grid=(bn, L // tq),
        in_specs=[pl.BlockSpec((1, tq, H_B * DH_B), row),
                  pl.BlockSpec((1, tq, H_IDX * LANES), row),
                  pl.BlockSpec((1, tq, LANES), row),
                  pl.BlockSpec((1, L, LANES), full),
                  pl.BlockSpec((1, L, R_KV), full),
                  pl.BlockSpec(wuk.shape, c3),
                  pl.BlockSpec(wuv.shape, c3),
                  pl.BlockSpec(ftab.shape, lambda b, i: (0, 0))],
        out_specs=pl.BlockSpec((1, tq, H_B * DH_B), row),
        out_shape=jax.ShapeDtypeStruct((bn, L, H_B * DH_B), out_dtype),
        scratch_shapes=[pltpu.VMEM((H_B, LANES, ftab.shape[1]), F32),
                        pltpu.VMEM((L, 2 * LANES), BF16),
                        pltpu.VMEM((L, R_KV + LANES), BF16),
                        pltpu.VMEM((R_KV + 16, L), BF16),
                        pltpu.VMEM((L, tq), jnp.int32),
                        pltpu.VMEM((8, tq), jnp.int32),
                        pltpu.VMEM((8, tq), jnp.int32),
                        pltpu.VMEM((H_B, 8, tq), F32),
                        pltpu.VMEM((H_B, R_KV + 16, tq), F32),
                        pltpu.VMEM((DSA_KC, tq), F32),
                        pltpu.VMEM((DSA_KC, tq), F32),
                        pltpu.VMEM((DSA_KC, tq), F32),
                        pltpu.VMEM((DSA_KC, tq), BF16),
                        pltpu.VMEM((DSA_KC, tq), BF16),
                        pltpu.VMEM((H_B, tq, R_KV), BF16),
                        pltpu.VMEM((H_IDX, tq, 2 * LANES), BF16),
                        pltpu.VMEM((H_B, 8, tq), F32)],
        compiler_params=_cparams("arbitrary", "arbitrary"),
        name="dsa_mixer",
    )(qb, qi, gates, ki, ckv, wuk, wuv, ftab)


def _router_gates(lg):
    lane = lax.broadcasted_iota(jnp.int32, lg.shape, 1)
    ninf = jnp.float32(-jnp.inf)
    lc = jnp.where(lane < N_GROUPS, lg, ninf)
    mc = jnp.max(lc, axis=1, keepdims=True)
    gidx = jnp.min(jnp.where(lc == mc, lane, LANES), axis=1, keepdims=True)
    p_g = 1.0 / jnp.sum(jnp.exp(lc - mc), axis=1, keepdims=True)
    lo = N_GROUPS + E_PER_GROUP * gidx
    lf = jnp.where((lane >= lo) & (lane < lo + E_PER_GROUP), lg, ninf)
    v1 = jnp.max(lf, axis=1, keepdims=True)
    i1 = jnp.min(jnp.where(lf == v1, lane, LANES), axis=1, keepdims=True)
    lf2 = jnp.where(lane == i1, ninf, lf)
    v2 = jnp.max(lf2, axis=1, keepdims=True)
    i2 = jnp.min(jnp.where(lf2 == v2, lane, LANES), axis=1, keepdims=True)
    e2 = jnp.exp(v2 - v1)
    p1 = 1.0 / (1.0 + e2)
    p2 = e2 / (1.0 + e2)
    return p_g * jnp.where(lane == i1, p1, jnp.where(lane == i2, p2, 0.0))


def _moe_dense_kernel(x_ref, g_ref, sc_ref, sh_ref, gt_ref, wr_ref, br_ref, wg_ref, wu_ref, wd_ref, gf_ref,
                      o_ref, hf, gates, acc, *, final):
    e = pl.program_id(2)

    @pl.when(e == 0)
    def _():
        h = _modulated_norm(x_ref[0], g_ref[...], sc_ref[0], sh_ref[0])
        hf[...] = h.astype(BF16)
        gates[...] = _router_gates(_fdot(h, wr_ref[...]) + br_ref[...])
        acc[...] = jnp.zeros(acc.shape, F32)

    lane = lax.broadcasted_iota(jnp.int32, gates.shape, 1)
    gcol = jnp.sum(jnp.where(lane == N_GROUPS + e, gates[...], 0.0), axis=1, keepdims=True)
    hb = hf[...]
    a = jnp.dot(hb, wg_ref[0], preferred_element_type=F32)
    a = a * jax.nn.sigmoid(a) * jnp.dot(hb, wu_ref[0], preferred_element_type=F32) * gcol
    acc[...] += _bdot(a, wd_ref[0])

    @pl.when(e == pl.num_programs(2) - 1)
    def _():
        y = x_ref[0] + gt_ref[0] * acc[...]
        if final:
            y = y * lax.rsqrt(jnp.mean(y * y, axis=-1, keepdims=True) + EPS) * gf_ref[...]
        o_ref[0] = y


def moe_layer(x, g, sc, sh, gt, w_coarse, b_coarse, w_fine, b_fine, w_gate, w_up, w_down, g_final=None, tm=1024):
    bn, L, d = x.shape
    ne = N_GROUPS * E_PER_GROUP
    ff = w_gate.shape[-1]
    wr = jnp.concatenate([w_coarse, jnp.transpose(w_fine, (1, 0, 2)).reshape(d, ne)], axis=1)
    wr = _pad_cols(wr, LANES)
    br = _pad_cols(jnp.concatenate([b_coarse, b_fine.reshape(-1)])[None, :], LANES)
    wg = w_gate.reshape(ne, d, ff).astype(BF16)
    wu = w_up.reshape(ne, d, ff).astype(BF16)
    wd = w_down.reshape(ne, ff, d).astype(BF16)
    final = g_final is not None
    gf = (g_final if final else jnp.ones((d,), F32)).reshape(1, d)
    row = lambda b, i, e: (b, i, 0)
    per_b = lambda b, i, e: (b, 0, 0)
    const = lambda b, i, e: (0, 0)
    per_e = lambda b, i, e: (e, 0, 0)
    kern = functools.partial(_moe_dense_kernel, final=final)
    return pl.pallas_call(
        kern,
        grid=(bn, L // tm, ne),
        in_specs=[pl.BlockSpec((1, tm, d), row),
                  pl.BlockSpec((1, d), const),
                  pl.BlockSpec((1, 1, d), per_b),
                  pl.BlockSpec((1, 1, d), per_b),
                  pl.BlockSpec((1, 1, d), per_b),
                  pl.BlockSpec((d, LANES), const),
                  pl.BlockSpec((1, LANES), const),
                  pl.BlockSpec((1, d, ff), per_e),
                  pl.BlockSpec((1, d, ff), per_e),
                  pl.BlockSpec((1, ff, d), per_e),
                  pl.BlockSpec((1, d), const)],
        out_specs=pl.BlockSpec((1, tm, d), row),
        out_shape=jax.ShapeDtypeStruct((bn, L, d), F32),
        scratch_shapes=[pltpu.VMEM((tm, d), BF16), pltpu.VMEM((tm, LANES), F32), pltpu.VMEM((tm, d), F32)],
        compiler_params=_cparams("parallel", "parallel", "arbitrary"),
        name="moe_layer",
    )(x, g.reshape(1, d), sc.reshape(bn, 1, d), sh.reshape(bn, 1, d), gt.reshape(bn, 1, d), wr, br, wg, wu, wd, gf)


def _pad_cols(w, width):
    return jnp.pad(w, ((0, 0), (0, width - w.shape[1])))


def _segments(widths):
    offs = np.cumsum((0,) + tuple(widths))
    return tuple((int(o), int(w)) for o, w in zip(offs[:-1], widths))


AB_WIDTHS = (2 * H_A * DK_A, H_A * DV_A, H_A * DV_A, LANES, H_B * DH_B, R_KV, H_IDX * LANES, LANES)
AB_DTYPES = (F32, BF16, F32, F32, BF16, BF16, F32, F32)
CD_WIDTHS = (2 * H_C * DK_C, H_C * DV_C, H_C * DV_C, LANES, H_D * DH_D, H_D * DH_D, H_D * DH_D)
CD_DTYPES = (F32, BF16, F32, F32, F32, F32, F32)


def pack_ab_w_in(w):
    d = w.shape[0]
    nqk = 2 * H_A * DK_A
    nv = H_A * DV_A
    o_i = nqk + 2 * nv
    o_qb = o_i + 2 * H_A
    o_ckv = o_qb + H_B * DH_B
    o_qi = o_ckv + R_KV
    o_ki = o_qi + H_IDX * D_IDX
    o_wi = o_ki + D_IDX
    gates = _pad_cols(jnp.concatenate([w[:, o_i:o_qb], w[:, o_wi:o_wi + H_IDX]], axis=1), LANES)
    wqi = w[:, o_qi:o_ki].reshape(d, H_IDX, D_IDX)
    wqi = jnp.concatenate([wqi, wqi], axis=-1).reshape(d, H_IDX * LANES)
    wki = w[:, o_ki:o_wi]
    return jnp.concatenate([w[:, :o_i], gates, w[:, o_qb:o_ckv], w[:, o_ckv:o_qi], wqi, wki, wki],
                           axis=1).astype(BF16)


def pack_cd_w_in(w):
    o_g = 2 * H_C * DK_C + H_C * DV_C
    o_r = o_g + GLA_RANK
    o_q = o_r + H_C * DV_C
    return jnp.concatenate([w[:, :o_g], w[:, o_r:o_q], _pad_cols(w[:, o_g:o_r], LANES), w[:, o_q:]],
                           axis=1).astype(BF16)


def kernel(x, c, w_ada, b_ada, g_mix, g_ffn, g_final, rel_bias, ab_w_in, ab_conv_w, ab_conv_b, ab_gate_b, ab_hnorm_g, ab_w_uk, ab_w_uv, ab_w_out, cd_w_in, cd_w_alpha, cd_b_alpha, cd_hnorm_g, cd_w_out, moe_w_coarse, moe_b_coarse, moe_w_fine, moe_b_fine, moe_w_gate, moe_w_up, moe_w_down):
    depth = w_ada.shape[0]
    L, d = x.shape[1], x.shape[2]
    mod = ada_mod(c, w_ada, b_ada)
    table = bias_table(rel_bias, L)
    for l in range(depth):
        sh_m, sc_m, gt_m, sh_f, sc_f, gt_f = [mod[l, :, i * d:(i + 1) * d] for i in range(6)]
        j = l // 2
        if l % 2 == 0:
            qk, va, oa, gates, qb, ckv, qi, ki = norm_proj(
                x, g_mix[l], sc_m, sh_m, pack_ab_w_in(ab_w_in[j]), _segments(AB_WIDTHS), AB_DTYPES)
            y1 = mlstm_mixer(qk, va, oa, gates, ab_conv_w[j], ab_conv_b[j], ab_gate_b[j], ab_hnorm_g[j],
                             out_dtype=BF16)
            y2 = dsa_mixer(qb, qi, gates, ki, ckv, ab_w_uk[j], ab_w_uv[j], table, out_dtype=BF16)
            w_out = ab_w_out[j]
        else:
            qk, vc, rc, gc, qd, kd, vd = norm_proj(
                x, g_mix[l], sc_m, sh_m, pack_cd_w_in(cd_w_in[j]), _segments(CD_WIDTHS), CD_DTYPES)
            y1 = gla_mixer(qk, vc, gc, rc, cd_w_alpha[j], cd_b_alpha[j], cd_hnorm_g[j], out_dtype=BF16)
            y2 = dilated_mixer(qd, kd, vd, rel_bias, out_dtype=BF16)
            w_out = cd_w_out[j]
        x = outproj_residual(x, y1, y2, gt_m, w_out.astype(BF16))
        x = moe_layer(x, g_ffn[l], sc_f, sh_f, gt_f, moe_w_coarse[l], moe_b_coarse[l], moe_w_fine[l],
                      moe_b_fine[l], moe_w_gate[l], moe_w_up[l], moe_w_down[l],
                      g_final=g_final if l == depth - 1 else None)
    return x
```

```python
import functools
import math

import numpy as np
import jax
import jax.numpy as jnp
from jax import lax
from jax.experimental import pallas as pl
from jax.experimental.pallas import tpu as pltpu

F32 = jnp.float32
BF16 = jnp.bfloat16

EPS = 1e-6
NEG_INF = -1e30

H_A, DK_A, DV_A, CONV_A, MLSTM_CHUNK = 4, 64, 128, 4, 64
H_B, DH_B, R_KV, H_IDX, D_IDX = 8, 64, 128, 4, 64
TOPK_MAX, TOPK_DIV = 256, 4
H_C, DK_C, DV_C, GLA_RANK, GLA_TAU, GLA_CHUNK = 4, 64, 128, 16, 16.0, 64
H_D, DH_D = 8, 64
DIL_PATTERNS = ((128, 1), (512, 4), (2048, 16))
N_BUCKETS, MAX_DIST = 32, 2048
N_GROUPS, E_PER_GROUP, EXPERT_FF = 4, 4, 512

LANES = 128
VMEM_LIMIT = 56 * 1024 * 1024


def _cparams(*sem):
    return pltpu.CompilerParams(dimension_semantics=sem, vmem_limit_bytes=VMEM_LIMIT)


def _bdot(a, b):
    return jnp.dot(a.astype(BF16), b.astype(BF16), preferred_element_type=F32)


def _bdot_nt(a, b):
    return lax.dot_general(a.astype(BF16), b.astype(BF16), (((1,), (1,)), ((), ())),
                           preferred_element_type=F32)


def _fdot(a, b):
    return jnp.dot(a, b, preferred_element_type=F32, precision=lax.Precision.HIGHEST)


def _ada_kernel(c_ref, w_ref, b_ref, o_ref):
    c = c_ref[...]
    a = c * jax.nn.sigmoid(c)
    o_ref[0] = _fdot(a, w_ref[0]) + b_ref[0]


def ada_mod(c, w_ada, b_ada):
    depth, d, e = w_ada.shape
    bn = c.shape[0]
    tn = 1536
    return pl.pallas_call(
        _ada_kernel,
        grid=(depth, e // tn),
        in_specs=[pl.BlockSpec((bn, d), lambda l, j: (0, 0)),
                  pl.BlockSpec((1, d, tn), lambda l, j: (l, 0, j)),
                  pl.BlockSpec((1, 1, tn), lambda l, j: (l, 0, j))],
        out_specs=pl.BlockSpec((1, bn, tn), lambda l, j: (l, 0, j)),
        out_shape=jax.ShapeDtypeStruct((depth, bn, e), F32),
        compiler_params=_cparams("arbitrary", "arbitrary"),
        name="ada_mod",
    )(c, w_ada, b_ada.reshape(depth, 1, e))


def _modulated_norm(x, g, sc, sh):
    y = x * lax.rsqrt(jnp.mean(x * x, axis=-1, keepdims=True) + EPS) * g
    return y * (1.0 + sc) + sh


def _norm_proj_kernel(x_ref, g_ref, sc_ref, sh_ref, w_ref, *o_refs, segs):
    h = _modulated_norm(x_ref[0], g_ref[...], sc_ref[0], sh_ref[0]).astype(BF16)
    for o_ref, (start, width) in zip(o_refs, segs):
        o_ref[0] = jnp.dot(h, w_ref[:, start:start + width], preferred_element_type=F32).astype(o_ref.dtype)


def norm_proj(x, g, sc, sh, w, segs, dtypes, tm=512):
    bn, L, d = x.shape
    wp = w.shape[1]
    kern = functools.partial(_norm_proj_kernel, segs=tuple(segs))
    return pl.pallas_call(
        kern,
        grid=(bn, L // tm),
        in_specs=[pl.BlockSpec((1, tm, d), lambda b, i: (b, i, 0)),
                  pl.BlockSpec((1, d), lambda b, i: (0, 0)),
                  pl.BlockSpec((1, 1, d), lambda b, i: (b, 0, 0)),
                  pl.BlockSpec((1, 1, d), lambda b, i: (b, 0, 0)),
                  pl.BlockSpec((d, wp), lambda b, i: (0, 0))],
        out_specs=[pl.BlockSpec((1, tm, wd), lambda b, i: (b, i, 0)) for (_, wd) in segs],
        out_shape=[jax.ShapeDtypeStruct((bn, L, wd), dt) for (_, wd), dt in zip(segs, dtypes)],
        compiler_params=_cparams("parallel", "parallel"),
        name="norm_proj",
    )(x, g.reshape(1, d), sc.reshape(bn, 1, d), sh.reshape(bn, 1, d), w)


def _outproj_kernel(x_ref, ya_ref, yb_ref, gt_ref, w_ref, o_ref):
    na = ya_ref.shape[-1]
    y = _bdot(ya_ref[0], w_ref[:na, :]) + _bdot(yb_ref[0], w_ref[na:, :])
    o_ref[0] = x_ref[0] + gt_ref[0] * y


def outproj_residual(x, ya, yb, gt, w, tm=512):
    bn, L, d = x.shape
    na, nb = ya.shape[-1], yb.shape[-1]
    return pl.pallas_call(
        _outproj_kernel,
        grid=(bn, L // tm),
        in_specs=[pl.BlockSpec((1, tm, d), lambda b, i: (b, i, 0)),
                  pl.BlockSpec((1, tm, na), lambda b, i: (b, i, 0)),
                  pl.BlockSpec((1, tm, nb), lambda b, i: (b, i, 0)),
                  pl.BlockSpec((1, 1, d), lambda b, i: (b, 0, 0)),
                  pl.BlockSpec((na + nb, d), lambda b, i: (0, 0))],
        out_specs=pl.BlockSpec((1, tm, d), lambda b, i: (b, i, 0)),
        out_shape=jax.ShapeDtypeStruct((bn, L, d), F32),
        compiler_params=_cparams("parallel", "parallel"),
        name="outproj_residual",
    )(x, ya, yb, gt.reshape(bn, 1, d), w)


def _log_sigmoid(x):
    return jnp.minimum(x, 0.0) - jnp.log1p(jnp.exp(-jnp.abs(x)))


def _dot_tn(a, b):
    return lax.dot_general(a.astype(BF16), b.astype(BF16), (((0,), (0,)), ((), ())),
                           preferred_element_type=F32)


def _tri(T):
    r = np.arange(T)
    return (r[:, None] >= r[None, :]).astype(np.float32)


SCAN_NB = 4


def _mlstm_kernel(qk_ref, v_ref, og_ref, gt_ref, cw_ref, cb_ref, gb_ref, hg_ref, ltri_ref, utri_ref,
                  o_ref, xbuf, cext, mstate):
    T = qk_ref.shape[1]
    tail = xbuf.shape[1] - T

    @pl.when(pl.program_id(1) == 0)
    def _():
        xbuf[:, 0:tail, :] = jnp.zeros((xbuf.shape[0], tail, xbuf.shape[2]), F32)
        cext[...] = jnp.zeros(cext.shape, F32)
        mstate[...] = jnp.full(mstate.shape, NEG_INF, F32)

    for bi in range(qk_ref.shape[0]):
        sl = pl.ds(bi, 1)
        _mlstm_chunk(qk_ref.at[sl], v_ref.at[sl], og_ref.at[sl], gt_ref.at[sl], cw_ref, cb_ref, gb_ref, hg_ref,
                     ltri_ref, utri_ref, o_ref.at[sl], xbuf.at[bi], cext.at[bi], mstate.at[bi])


def _mlstm_chunk(qk_ref, v_ref, og_ref, gt_ref, cw_ref, cb_ref, gb_ref, hg_ref, ltri_ref, utri_ref,
                 o_ref, xbuf, cext, mstate):
    T = qk_ref.shape[1]
    tail = xbuf.shape[0] - T
    x = qk_ref[0]
    xbuf[tail:tail + T, :] = x
    conv = cb_ref[...] + sum(xbuf[pl.ds(tail - (CONV_A - 1) + j, T), :] * cw_ref[j:j + 1, :]
                             for j in range(CONV_A))
    xbuf[0:tail, :] = x[T - tail:T, :]
    qk = conv * jax.nn.sigmoid(conv)

    G = gt_ref[0] + gb_ref[...]
    GT = G.T
    bcol = _fdot(ltri_ref[...], _log_sigmoid(G))
    brow = _fdot(_log_sigmoid(GT), utri_ref[...])
    causal = ltri_ref[...] > 0.5
    lane = lax.broadcasted_iota(jnp.int32, (T, DV_A), 1)
    ones_blk = jnp.where(lane == 0, 1.0, 0.0).astype(F32)
    nq = H_A * DK_A

    for h in range(H_A):
        b_c = bcol[:, H_A + h:H_A + h + 1]
        b_r = brow[H_A + h:H_A + h + 1, :]
        li_r = GT[h:h + 1, :]
        li_c = G[:, h:h + 1]
        m_prev = mstate[h:h + 1, 0:1]
        Dm = jnp.where(causal, b_c - b_r + li_r, NEG_INF)
        g = b_c + m_prev
        mt = jnp.maximum(g, jnp.max(Dm, axis=1, keepdims=True))
        w_intra = jnp.exp(Dm - mt)
        w_state = jnp.exp(g - mt)
        q = qk[:, h * DK_A:(h + 1) * DK_A] * (DK_A ** -0.5)
        k = qk[:, nq + h * DK_A:nq + (h + 1) * DK_A]
        vext = jnp.concatenate([v_ref[0, :, h * DV_A:(h + 1) * DV_A].astype(F32), ones_blk], axis=1)
        s = _bdot_nt(q, k) * w_intra
        nd = w_state * _bdot(q, cext[h]) + _bdot(s, vext)
        num = nd[:, :DV_A]
        den = nd[:, DV_A:DV_A + 1]
        hout = num / jnp.maximum(jnp.abs(den), jnp.exp(-mt))
        bL = b_c[T - 1:T, :]
        d_end = bL - b_c + li_c
        m_new = jnp.maximum(bL + m_prev, jnp.max(d_end, axis=0, keepdims=True))
        w_e = jnp.exp(d_end - m_new)
        decay = jnp.exp(bL + m_prev - m_new)
        cext[h] = decay * cext[h] + _dot_tn(k * w_e, vext)
        mstate[h:h + 1, :] = jnp.broadcast_to(m_new, (1, mstate.shape[1]))
        y = hout * lax.rsqrt(jnp.mean(hout * hout, axis=-1, keepdims=True) + EPS)
        y = y * hg_ref[:, h * DV_A:(h + 1) * DV_A]
        o_ref[0, :, h * DV_A:(h + 1) * DV_A] = (
            y * jax.nn.sigmoid(og_ref[0, :, h * DV_A:(h + 1) * DV_A])).astype(o_ref.dtype)


def mlstm_mixer(qk, v, og, gates, conv_w, conv_b, gate_b, hnorm_g, out_dtype=F32):
    bn, L, _ = qk.shape
    T = MLSTM_CHUNK
    nv = H_A * DV_A
    gb = jnp.zeros((1, LANES), F32).at[0, :2 * H_A].set(gate_b.reshape(-1))
    ltri = jnp.asarray(_tri(T))
    row = lambda b, i: (b, i, 0)
    const = lambda b, i: (0, 0)
    nb = SCAN_NB if bn % SCAN_NB == 0 else 1
    return pl.pallas_call(
        _mlstm_kernel,
        grid=(bn // nb, L // T),
        in_specs=[pl.BlockSpec((nb, T, qk.shape[-1]), row),
                  pl.BlockSpec((nb, T, nv), row),
                  pl.BlockSpec((nb, T, nv), row),
                  pl.BlockSpec((nb, T, LANES), row),
                  pl.BlockSpec(conv_w.shape, const),
                  pl.BlockSpec((1, conv_b.shape[-1]), const),
                  pl.BlockSpec((1, LANES), const),
                  pl.BlockSpec((1, nv), const),
                  pl.BlockSpec((T, T), const),
                  pl.BlockSpec((T, T), const)],
        out_specs=pl.BlockSpec((nb, T, nv), row),
        out_shape=jax.ShapeDtypeStruct((bn, L, nv), out_dtype),
        scratch_shapes=[pltpu.VMEM((nb, 8 + T, qk.shape[-1]), F32),
                        pltpu.VMEM((nb, H_A, DK_A, 2 * DV_A), F32),
                        pltpu.VMEM((nb, 8, LANES), F32)],
        compiler_params=_cparams("parallel", "arbitrary"),
        name="mlstm_mixer",
    )(qk, v, og, gates, conv_w, conv_b.reshape(1, -1), gb, hnorm_g.reshape(1, -1), ltri, ltri.T)


def _gla_level_consts(T):
    r = np.arange(T)
    mq, mk, bm = [], [], []
    n = T
    while n >= 2:
        start = (r // n) * n
        mid = start + n // 2
        upper = r >= mid
        mq.append((upper[:, None] & (r[None, :] >= mid[:, None]) & (r[None, :] <= r[:, None])).astype(np.float32))
        mk.append((~upper[:, None] & (r[None, :] > r[:, None]) & (r[None, :] < mid[:, None])).astype(np.float32))
        bm.append((upper[:, None] & ~upper[None, :] & (start[:, None] == start[None, :])).astype(np.float32))
        n //= 2
    bm.append(np.eye(T, dtype=np.float32))
    return np.concatenate(mq, 0), np.concatenate(mk, 0), np.stack(bm, 0)


def _gla_kernel(qk_ref, v_ref, gc_ref, rc_ref, wa_ref, ba_ref, hg_ref, ltri_ref, mq_ref, mk_ref, bm_ref,
                o_ref, st):
    @pl.when(pl.program_id(1) == 0)
    def _():
        st[...] = jnp.zeros(st.shape, F32)

    for bi in range(qk_ref.shape[0]):
        sl = pl.ds(bi, 1)
        _gla_chunk(qk_ref.at[sl], v_ref.at[sl], gc_ref.at[sl], rc_ref.at[sl], wa_ref, ba_ref, hg_ref, ltri_ref,
                   mq_ref, mk_ref, bm_ref, o_ref.at[sl], st.at[bi])


def _gla_chunk(qk_ref, v_ref, gc_ref, rc_ref, wa_ref, ba_ref, hg_ref, ltri_ref, mq_ref, mk_ref, bm_ref,
               o_ref, st):
    T = qk_ref.shape[1]
    nlev = bm_ref.shape[0] - 1
    nq = H_C * DK_C

    qk = qk_ref[0]
    q = qk[:, :nq] * (DK_C ** -0.5)
    k = qk[:, nq:]
    la = _log_sigmoid(_bdot(gc_ref[0], wa_ref[...]) + ba_ref[...]) * (1.0 / GLA_TAU)
    eq = jnp.exp(_fdot(mq_ref[...], la))
    ek = jnp.exp(_fdot(mk_ref[...], la))
    bfull = _fdot(ltri_ref[...], la)
    bL = bfull[T - 1:T, :]
    qe = q * jnp.exp(bfull)
    kd = k * jnp.exp(bL - bfull)
    eL = jnp.exp(bL)

    for h in range(H_C):
        cs = slice(h * DK_C, (h + 1) * DK_C)
        vs = slice(h * DV_C, (h + 1) * DV_C)
        qh, kh, vh = q[:, cs], k[:, cs], v_ref[0, :, vs]
        A = bm_ref[nlev] * _bdot_nt(qh, kh)
        for lv in range(nlev):
            A = A + bm_ref[lv] * _bdot_nt(qh * eq[lv * T:(lv + 1) * T, cs], kh * ek[lv * T:(lv + 1) * T, cs])
        o = _bdot_nt(qe[:, cs], st[h]) + _bdot(A, vh)
        st[h] = eL[:, cs] * st[h] + _dot_tn(vh, kd[:, cs])
        y = o * lax.rsqrt(jnp.mean(o * o, axis=-1, keepdims=True) + EPS) * hg_ref[:, vs]
        r = rc_ref[0, :, vs]
        o_ref[0, :, vs] = (y * (r * jax.nn.sigmoid(r))).astype(o_ref.dtype)


def gla_mixer(qk, v, gc, rc, w_alpha, b_alpha, hnorm_g, out_dtype=F32):
    bn, L, _ = qk.shape
    T = GLA_CHUNK
    nv = H_C * DV_C
    nq = H_C * DK_C
    wa = jnp.pad(w_alpha, ((0, gc.shape[-1] - w_alpha.shape[0]), (0, 0))).astype(BF16)
    mq, mk, bm = _gla_level_consts(T)
    row = lambda b, i: (b, i, 0)
    const = lambda b, i: (0, 0)
    nb = SCAN_NB if bn % SCAN_NB == 0 else 1
    return pl.pallas_call(
        _gla_kernel,
        grid=(bn // nb, L // T),
        in_specs=[pl.BlockSpec((nb, T, 2 * nq), row),
                  pl.BlockSpec((nb, T, nv), row),
                  pl.BlockSpec((nb, T, gc.shape[-1]), row),
                  pl.BlockSpec((nb, T, nv), row),
                  pl.BlockSpec(wa.shape, const),
                  pl.BlockSpec((1, nq), const),
                  pl.BlockSpec((1, nv), const),
                  pl.BlockSpec((T, T), const),
                  pl.BlockSpec(mq.shape, const),
                  pl.BlockSpec(mk.shape, const),
                  pl.BlockSpec(bm.shape, lambda b, i: (0, 0, 0))],
        out_specs=pl.BlockSpec((nb, T, nv), row),
        out_shape=jax.ShapeDtypeStruct((bn, L, nv), out_dtype),
        scratch_shapes=[pltpu.VMEM((nb, H_C, DV_C, DK_C), F32)],
        compiler_params=_cparams("parallel", "arbitrary"),
        name="gla_mixer",
    )(qk, v, gc, rc, wa, b_alpha.reshape(1, nq), hnorm_g.reshape(1, nv), jnp.asarray(_tri(T)),
      jnp.asarray(mq), jnp.asarray(mk), jnp.asarray(bm))


def _rel_bucket(dist):
    n = jnp.maximum(dist, 0)
    max_exact = N_BUCKETS // 2
    nf = jnp.maximum(n, 1).astype(F32)
    large = max_exact + (jnp.log(nf / max_exact) / math.log(MAX_DIST / max_exact)
                         * (N_BUCKETS - max_exact)).astype(jnp.int32)
    large = jnp.minimum(large, N_BUCKETS - 1)
    return jnp.where(n < max_exact, n, large)


def _bias_lookup_kernel(bk_ref, rb_ref, o_ref):
    n = bk_ref.shape[0]
    onehot = (lax.broadcasted_iota(jnp.int32, (n, LANES), 1) == bk_ref[...]).astype(F32)
    o_ref[...] = _fdot(onehot, rb_ref[...])


def bias_lookup(rel_bias, dist):
    n = dist.shape[0]
    tn = min(n, 4096)
    bk = jnp.broadcast_to(_rel_bucket(dist)[:, None], (n, LANES))
    rb = jnp.pad(rel_bias, ((0, LANES - rel_bias.shape[0]), (0, LANES - rel_bias.shape[1])))
    return pl.pallas_call(
        _bias_lookup_kernel,
        grid=(n // tn,),
        in_specs=[pl.BlockSpec((tn, LANES), lambda i: (i, 0)), pl.BlockSpec((LANES, LANES), lambda i: (0, 0))],
        out_specs=pl.BlockSpec((tn, LANES), lambda i: (i, 0)),
        out_shape=jax.ShapeDtypeStruct((n, LANES), F32),
        compiler_params=_cparams("parallel"),
        name="bias_lookup",
    )(bk, rb)


def bias_table(rel_bias, L):
    return bias_lookup(rel_bias, jnp.arange(L, dtype=jnp.int32))


DIL_W = 128
DIL_HB = 2


def _dilated_kernel(q_ref, kp_ref, kc_ref, vp_ref, vc_ref, bias_ref, o_ref, kall, vall, num, mx, den):
    W = DIL_W
    tb = q_ref.shape[1]
    first = pl.program_id(2) == 0
    kall[0:tb, :] = kp_ref[0]
    kall[tb:2 * tb, :] = kc_ref[0]
    vall[0:tb, :] = vp_ref[0]
    vall[tb:2 * tb, :] = vc_ref[0]
    num[...] = jnp.zeros(num.shape, F32)
    den[...] = jnp.zeros(den.shape, F32)
    mx[...] = jnp.full(mx.shape, NEG_INF, F32)
    col = lax.broadcasted_iota(jnp.int32, (1, 2 * W), 1)

    for pi, (window, dil) in enumerate(DIL_PATTERNS):
        span = W * dil

        def unit(u, carry, pi=pi, dil=dil, span=span):
            g = u // dil
            q0 = g * span + (u % dil)
            k0 = tb + q0 - span
            qu = q_ref[0, pl.ds(q0, W, stride=dil), :]
            ku = kall[pl.ds(k0, 2 * W, stride=dil), :]
            vu = vall[pl.ds(k0, 2 * W, stride=dil), :]
            edge = jnp.where((col < W) & first & (g == 0), NEG_INF, 0.0).astype(F32)
            for hh in range(DIL_HB):
                cs = slice(hh * DH_D, (hh + 1) * DH_D)
                logits = _bdot_nt(qu[:, cs] * (DH_D ** -0.5), ku[:, cs]) + bias_ref[pi, hh] + edge
                m = jnp.max(logits, axis=1, keepdims=True)
                p = jnp.exp(logits - m)
                nm = _bdot(p, vu[:, cs])
                dn = jnp.sum(p, axis=1, keepdims=True)
                rows = pl.ds(q0, W, stride=dil)
                m0 = mx[hh, rows, :]
                mm = jnp.maximum(m, m0)
                a, b = jnp.exp(m - mm), jnp.exp(m0 - mm)
                num[hh, rows, :] = nm * a + num[hh, rows, :] * b
                den[hh, rows, :] = dn * a + den[hh, rows, :] * b
                mx[hh, rows, :] = mm
            return carry

        lax.fori_loop(0, tb // W, unit, 0, unroll=4)

    for hh in range(DIL_HB):
        o_ref[0, :, hh * DH_D:(hh + 1) * DH_D] = (num[hh] / den[hh]).astype(o_ref.dtype)


def dilated_mixer(q, k, v, rel_bias, out_dtype=F32):
    bn, L, nd = q.shape
    W = DIL_W
    tb = W * max(d for _, d in DIL_PATTERNS)
    cw = DIL_HB * DH_D
    tap = np.arange(W)[:, None] + W - np.arange(2 * W)[None, :]
    ok = jnp.asarray((tap >= 0) & (tap <= W))[None]
    biases = []
    for window, dil in DIL_PATTERNS:
        assert window // dil == W
        tile = bias_lookup(rel_bias, jnp.asarray((np.clip(tap, 0, W) * dil).reshape(-1), jnp.int32))
        biases.append(jnp.where(ok, tile[:, :H_D].reshape(W, 2 * W, H_D).transpose(2, 0, 1), NEG_INF))
    bias = jnp.stack(biases)
    cur = lambda b, c, j: (b, j, c)
    prv = lambda b, c, j: (b, jnp.maximum(j - 1, 0), c)
    blk = (1, tb, cw)
    return pl.pallas_call(
        _dilated_kernel,
        grid=(bn, nd // cw, L // tb),
        in_specs=[pl.BlockSpec(blk, cur), pl.BlockSpec(blk, prv), pl.BlockSpec(blk, cur),
                  pl.BlockSpec(blk, prv), pl.BlockSpec(blk, cur),
                  pl.BlockSpec((len(DIL_PATTERNS), DIL_HB, W, 2 * W), lambda b, c, j: (0, c, 0, 0))],
        out_specs=pl.BlockSpec(blk, cur),
        out_shape=jax.ShapeDtypeStruct((bn, L, nd), out_dtype),
        scratch_shapes=[pltpu.VMEM((2 * tb, cw), F32), pltpu.VMEM((2 * tb, cw), F32),
                        pltpu.VMEM((DIL_HB, tb, DH_D), F32), pltpu.VMEM((DIL_HB, tb, DH_D), F32),
                        pltpu.VMEM((DIL_HB, tb, DH_D), F32)],
        compiler_params=_cparams("parallel", "parallel", "parallel"),
        name="dilated_mixer",
    )(q, k, k, v, v, bias)


INT_MIN = -2 ** 31
DSA_TQ = 256
DSA_KC = 512
DSA_AR = 64


def _split_hi_lo(x):
    hi = x.astype(BF16)
    lo = (x - hi.astype(F32)).astype(BF16)
    return hi, lo


def _dsa_kernel(qb_ref, qi_ref, gt_ref, ki_ref, ckv_ref, wuk_ref, wuv_ref, ftab_ref, o_ref,
                skew, kcat, cke, cket, skey, thr_ref, j_ref, m_ref, acc_ref, madd_ref, lg0_ref, lg1_ref,
                p0_ref, p1_ref, qabs_ref, qcat_ref, alpha_ref, *, topk):
    tq = qb_ref.shape[1]
    L = ki_ref.shape[1]
    kc = DSA_KC
    ar = DSA_AR
    i = pl.program_id(1)
    t0 = i * tq
    nck = (t0 + tq + kc - 1) // kc
    left = lax.broadcasted_iota(jnp.int32, (1, LANES), 1) < (LANES // 2)
    tpos = t0 + lax.broadcasted_iota(jnp.int32, (1, tq), 1)
    krow = lax.broadcasted_iota(jnp.int32, (kc, tq), 0)

    @pl.when((pl.program_id(0) == 0) & (i == 0))
    def _():
        ltot = ftab_ref.shape[1]
        for h in range(H_B):
            x = jnp.broadcast_to(ftab_ref[h:h + 1, :], (LANES, ltot))
            skew[h] = pltpu.roll(x, ltot - LANES, 1, stride=1, stride_axis=0)

    @pl.when(i == 0)
    def _():
        hi, lo = _split_hi_lo(ki_ref[0])
        kcat[:, :LANES] = hi
        kcat[:, LANES:] = jnp.where(left, lo, jnp.zeros_like(lo))
        cke[:, :R_KV] = ckv_ref[0]
        one_col = lax.broadcasted_iota(jnp.int32, (L, LANES), 1) == 0
        cke[:, R_KV:] = jnp.where(one_col, 1.0, 0.0).astype(BF16)
        nr = cket.shape[0]
        eye = (lax.broadcasted_iota(jnp.int32, (nr, R_KV + LANES), 0)
               == lax.broadcasted_iota(jnp.int32, (nr, R_KV + LANES), 1)).astype(BF16)
        for c in range(L // kc):
            cket[:, c * kc:(c + 1) * kc] = _bdot_nt(eye, cke[c * kc:(c + 1) * kc, :]).astype(BF16)

    gt_t = gt_ref[0].T
    ws = []
    for h in range(H_IDX):
        hi, lo = _split_hi_lo(qi_ref[0, :, h * LANES:(h + 1) * LANES])
        qcat_ref[h, :, :LANES] = jnp.where(left, hi, lo)
        qcat_ref[h, :, LANES:] = jnp.where(left, hi, jnp.zeros_like(hi))
        ws.append(gt_t[2 * H_A + h:2 * H_A + h + 1, :] * (H_IDX ** -0.5))

    def score_body(c, carry):
        s0 = pl.multiple_of(c * kc, kc)
        sc = jnp.zeros((kc, tq), F32)
        for h in range(H_IDX):
            sc = sc + ws[h] * jnp.maximum(lax.dot_general(kcat[pl.ds(s0, kc), :], qcat_ref[h],
                                                          (((1,), (1,)), ((), ())),
                                                          preferred_element_type=F32), 0.0)
        bits = lax.bitcast_convert_type(sc + 0.0, jnp.int32)
        key = jnp.where(bits < 0, bits ^ jnp.int32(0x7FFFFFFF), bits)
        skey[pl.ds(s0, kc), :] = jnp.where(s0 + krow <= tpos, key, jnp.int32(INT_MIN))
        return carry

    lax.fori_loop(0, nck, score_body, 0)

    arow = lax.broadcasted_iota(jnp.int32, (ar, tq), 0)

    def count(pred):
        def body(c, acc):
            for a in range(kc // ar):
                r0 = pl.multiple_of(c * kc + a * ar, ar)
                acc = acc + pred(skey[pl.ds(r0, ar), :], r0 + arow).astype(jnp.int32)
            return acc
        acc = lax.fori_loop(0, nck, body, jnp.zeros((ar, tq), jnp.int32))
        return jnp.sum(acc, axis=0, keepdims=True)

    thr_ref[...] = jnp.full(thr_ref.shape, INT_MIN, jnp.int32)
    j_ref[...] = jnp.full(j_ref.shape, L, jnp.int32)

    @pl.when(t0 + tq > topk)
    def _():
        def bit_body(it, tu):
            cand_u = tu | (jnp.int32(1) << (31 - it))
            cand = cand_u ^ jnp.int32(INT_MIN)
            return jnp.where(count(lambda blk, idx: blk >= cand) >= topk, cand_u, tu)

        tu = lax.fori_loop(0, 32, bit_body, jnp.zeros((1, tq), jnp.int32))
        thr = tu ^ jnp.int32(INT_MIN)
        thr_ref[...] = jnp.broadcast_to(thr, thr_ref.shape)
        need = topk - count(lambda blk, idx: blk > thr)
        n_eq = count(lambda blk, idx: blk == thr)
        tied = ((n_eq > need) & (thr != INT_MIN)).astype(jnp.int32)

        @pl.when(jnp.max(tied) > 0)
        def _():
            nbits = max(1, (L - 1).bit_length())

            def jbit_body(it, jv):
                cand = jv | (jnp.int32(1) << (nbits - 1 - it))
                return jnp.where(count(lambda blk, idx: (blk == thr) & (idx < cand)) < need, cand, jv)

            jv = lax.fori_loop(0, nbits, jbit_body, jnp.zeros((1, tq), jnp.int32))
            j_ref[...] = jnp.broadcast_to(jv, j_ref.shape)

    thr = thr_ref[0:1, :]
    jv = j_ref[0:1, :]
    for h in range(H_B):
        qabs_ref[h] = (_bdot_nt(qb_ref[0, :, h * DH_B:(h + 1) * DH_B], wuk_ref[h]) * (DH_B ** -0.5)).astype(BF16)
    m_ref[...] = jnp.full(m_ref.shape, NEG_INF, F32)
    acc_ref[...] = jnp.zeros(acc_ref.shape, F32)

    sb = LANES
    nsb = kc // sb
    srow = lax.broadcasted_iota(jnp.int32, (sb, tq), 0)

    def attn_body(c, carry):
        s0 = pl.multiple_of(c * kc, kc)
        for a in range(nsb):
            blk = skey[pl.ds(s0 + a * sb, sb), :]
            idx = s0 + a * sb + srow
            sel = ((blk > thr) | ((blk == thr) & (idx <= jv))) & (idx <= tpos)
            madd_ref[a * sb:(a + 1) * sb, :] = jnp.where(sel, 0.0, NEG_INF).astype(F32)
        def scores(h, lg):
            lg[...] = _bdot_nt(cke[pl.ds(s0, kc), :R_KV], qabs_ref[h])

        def softmax(h, lg, pb):
            m_old = m_ref[h, 0:1, :]
            m_new = m_old
            for a in range(nsb):
                rows = slice(a * sb, (a + 1) * sb)
                off = pl.multiple_of(t0 - s0 + kc - (a + 1) * sb, sb)
                x = lg[rows, :] + skew[h, :, pl.ds(off, tq)] + madd_ref[rows, :]
                lg[rows, :] = x
                m_new = jnp.maximum(m_new, jnp.max(x, axis=0, keepdims=True))
            for a in range(nsb):
                rows = slice(a * sb, (a + 1) * sb)
                pb[rows, :] = jnp.exp(lg[rows, :] - m_new).astype(BF16)
            alpha_ref[h] = jnp.broadcast_to(jnp.exp(m_old - m_new), alpha_ref.shape[1:])
            m_ref[h] = jnp.broadcast_to(m_new, m_ref.shape[1:])

        def accumulate(h, pb):
            acc_ref[h] = alpha_ref[h, 0:1, :] * acc_ref[h] + jnp.dot(cket[:, pl.ds(s0, kc)], pb[...],
                                                                    preferred_element_type=F32)

        scores(0, lg0_ref)
        scores(1, lg1_ref)
        softmax(0, lg0_ref, p0_ref)
        scores(2, lg0_ref)
        accumulate(0, p0_ref)
        softmax(1, lg1_ref, p1_ref)

        def pair_body(k, carry2):
            e = 2 * k
            scores(e + 1, lg1_ref)
            accumulate(e - 1, p1_ref)
            softmax(e, lg0_ref, p0_ref)
            scores((e + 2) % H_B, lg0_ref)
            accumulate(e, p0_ref)
            softmax(e + 1, lg1_ref, p1_ref)
            return carry2

        lax.fori_loop(1, H_B // 2, pair_body, 0)
        accumulate(H_B - 1, p1_ref)
        return carry

    lax.fori_loop(0, nck, attn_body, 0)
    for h in range(H_B):
        o_lat = acc_ref[h, :R_KV, :] / acc_ref[h, R_KV:R_KV + 1, :]
        o_ref[0, :, h * DH_B:(h + 1) * DH_B] = _dot_tn(o_lat, wuv_ref[h]).astype(o_ref.dtype)


def dsa_mixer(qb, qi, gates, ki, ckv, w_uk, w_uv, table, out_dtype=F32):
    bn, L, _ = qb.shape
    tq = min(DSA_TQ, L)
    topk = min(TOPK_MAX, L // TOPK_DIV)
    wuk = jnp.transpose(w_uk, (1, 0, 2)).astype(BF16)
    wuv = jnp.transpose(w_uv, (1, 0, 2)).astype(BF16)
    ftab = jnp.concatenate([jnp.zeros((8, DSA_KC), F32), table[:, :8].T], axis=1)
    row = lambda b, i: (b, i, 0)
    full = lambda b, i: (b, 0, 0)
    c3 = lambda b, i: (0, 0, 0)
    kern = functools.partial(_dsa_kernel, topk=topk)
    return pl.pallas_call(
        kern,
        grid=(bn, L // tq),
        in_specs=[pl.BlockSpec((1, tq, H_B * DH_B), row),
                  pl.BlockSpec((1, tq, H_IDX * LANES), row),
                  pl.BlockSpec((1, tq, LANES), row),
                  pl.BlockSpec((1, L, LANES), full),
                  pl.BlockSpec((1, L, R_KV), full),
                  pl.BlockSpec(wuk.shape, c3),
                  pl.BlockSpec(wuv.shape, c3),
                  pl.BlockSpec(ftab.shape, lambda b, i: (0, 0))],
        out_specs=pl.BlockSpec((1, tq, H_B * DH_B), row),
        out_shape=jax.ShapeDtypeStruct((bn, L, H_B * DH_B), out_dtype),
        scratch_shapes=[pltpu.VMEM((H_B, LANES, ftab.shape[1]), F32),
                        pltpu.VMEM((L, 2 * LANES), BF16),
                        pltpu.VMEM((L, R_KV + LANES), BF16),
                        pltpu.VMEM((R_KV + 16, L), BF16),
                        pltpu.VMEM((L, tq), jnp.int32),
                        pltpu.VMEM((8, tq), jnp.int32),
                        pltpu.VMEM((8, tq), jnp.int32),
                        pltpu.VMEM((H_B, 8, tq), F32),
                        pltpu.VMEM((H_B, R_KV + 16, tq), F32),
                        pltpu.VMEM((DSA_KC, tq), F32),
                        pltpu.VMEM((DSA_KC, tq), F32),
                        pltpu.VMEM((DSA_KC, tq), F32),
                        pltpu.VMEM((DSA_KC, tq), BF16),
                        pltpu.VMEM((DSA_KC, tq), BF16),
                        pltpu.VMEM((H_B, tq, R_KV), BF16),
                        pltpu.VMEM((H_IDX, tq, 2 * LANES), BF16),
                        pltpu.VMEM((H_B, 8, tq), F32)],
        compiler_params=_cparams("arbitrary", "arbitrary"),
        name="dsa_mixer",
    )(qb, qi, gates, ki, ckv, wuk, wuv, ftab)


def _router_gates(lg):
    lane = lax.broadcasted_iota(jnp.int32, lg.shape, 1)
    ninf = jnp.float32(-jnp.inf)
    lc = jnp.where(lane < N_GROUPS, lg, ninf)
    mc = jnp.max(lc, axis=1, keepdims=True)
    gidx = jnp.min(jnp.where(lc == mc, lane, LANES), axis=1, keepdims=True)
    p_g = 1.0 / jnp.sum(jnp.exp(lc - mc), axis=1, keepdims=True)
    lo = N_GROUPS + E_PER_GROUP * gidx
    lf = jnp.where((lane >= lo) & (lane < lo + E_PER_GROUP), lg, ninf)
    v1 = jnp.max(lf, axis=1, keepdims=True)
    i1 = jnp.min(jnp.where(lf == v1, lane, LANES), axis=1, keepdims=True)
    lf2 = jnp.where(lane == i1, ninf, lf)
    v2 = jnp.max(lf2, axis=1, keepdims=True)
    i2 = jnp.min(jnp.where(lf2 == v2, lane, LANES), axis=1, keepdims=True)
    e2 = jnp.exp(v2 - v1)
    p1 = 1.0 / (1.0 + e2)
    p2 = e2 / (1.0 + e2)
    return p_g * jnp.where(lane == i1, p1, jnp.where(lane == i2, p2, 0.0))


def _moe_dense_kernel(x_ref, g_ref, sc_ref, sh_ref, gt_ref, wr_ref, br_ref, wg_ref, wu_ref, wd_ref, gf_ref,
                      o_ref, hf, gates, acc, *, final):
    e = pl.program_id(2)

    @pl.when(e == 0)
    def _():
        h = _modulated_norm(x_ref[0], g_ref[...], sc_ref[0], sh_ref[0])
        hf[...] = h.astype(BF16)
        gates[...] = _router_gates(_fdot(h, wr_ref[...]) + br_ref[...])
        acc[...] = jnp.zeros(acc.shape, F32)

    lane = lax.broadcasted_iota(jnp.int32, gates.shape, 1)
    gcol = jnp.sum(jnp.where(lane == N_GROUPS + e, gates[...], 0.0), axis=1, keepdims=True)
    hb = hf[...]
    a = jnp.dot(hb, wg_ref[0], preferred_element_type=F32)
    a = a * jax.nn.sigmoid(a) * jnp.dot(hb, wu_ref[0], preferred_element_type=F32) * gcol
    acc[...] += _bdot(a, wd_ref[0])

    @pl.when(e == pl.num_programs(2) - 1)
    def _():
        y = x_ref[0] + gt_ref[0] * acc[...]
        if final:
            y = y * lax.rsqrt(jnp.mean(y * y, axis=-1, keepdims=True) + EPS) * gf_ref[...]
        o_ref[0] = y


def moe_layer(x, g, sc, sh, gt, w_coarse, b_coarse, w_fine, b_fine, w_gate, w_up, w_down, g_final=None, tm=1024):
    bn, L, d = x.shape
    ne = N_GROUPS * E_PER_GROUP
    ff = w_gate.shape[-1]
    wr = jnp.concatenate([w_coarse, jnp.transpose(w_fine, (1, 0, 2)).reshape(d, ne)], axis=1)
    wr = _pad_cols(wr, LANES)
    br = _pad_cols(jnp.concatenate([b_coarse, b_fine.reshape(-1)])[None, :], LANES)
    wg = w_gate.reshape(ne, d, ff).astype(BF16)
    wu = w_up.reshape(ne, d, ff).astype(BF16)
    wd = w_down.reshape(ne, ff, d).astype(BF16)
    final = g_final is not None
    gf = (g_final if final else jnp.ones((d,), F32)).reshape(1, d)
    row = lambda b, i, e: (b, i, 0)
    per_b = lambda b, i, e: (b, 0, 0)
    const = lambda b, i, e: (0, 0)
    per_e = lambda b, i, e: (e, 0, 0)
    kern = functools.partial(_moe_dense_kernel, final=final)
    return pl.pallas_call(
        kern,
        grid=(bn, L // tm, ne),
        in_specs=[pl.BlockSpec((1, tm, d), row),
                  pl.BlockSpec((1, d), const),
                  pl.BlockSpec((1, 1, d), per_b),
                  pl.BlockSpec((1, 1, d), per_b),
                  pl.BlockSpec((1, 1, d), per_b),
                  pl.BlockSpec((d, LANES), const),
                  pl.BlockSpec((1, LANES), const),
                  pl.BlockSpec((1, d, ff), per_e),
                  pl.BlockSpec((1, d, ff), per_e),
                  pl.BlockSpec((1, ff, d), per_e),
                  pl.BlockSpec((1, d), const)],
        out_specs=pl.BlockSpec((1, tm, d), row),
        out_shape=jax.ShapeDtypeStruct((bn, L, d), F32),
        scratch_shapes=[pltpu.VMEM((tm, d), BF16), pltpu.VMEM((tm, LANES), F32), pltpu.VMEM((tm, d), F32)],
        compiler_params=_cparams("parallel", "parallel", "arbitrary"),
        name="moe_layer",
    )(x, g.reshape(1, d), sc.reshape(bn, 1, d), sh.reshape(bn, 1, d), gt.reshape(bn, 1, d), wr, br, wg, wu, wd, gf)


def _pad_cols(w, width):
    return jnp.pad(w, ((0, 0), (0, width - w.shape[1])))


def _segments(widths):
    offs = np.cumsum((0,) + tuple(widths))
    return tuple((int(o), int(w)) for o, w in zip(offs[:-1], widths))


AB_WIDTHS = (2 * H_A * DK_A, H_A * DV_A, H_A * DV_A, LANES, H_B * DH_B, R_KV, H_IDX * LANES, LANES)
AB_DTYPES = (F32, BF16, F32, F32, BF16, BF16, F32, F32)
CD_WIDTHS = (2 * H_C * DK_C, H_C * DV_C, H_C * DV_C, LANES, H_D * DH_D, H_D * DH_D, H_D * DH_D)
CD_DTYPES = (F32, BF16, F32, F32, F32, F32, F32)


def pack_ab_w_in(w):
    d = w.shape[0]
    nqk = 2 * H_A * DK_A
    nv = H_A * DV_A
    o_i = nqk + 2 * nv
    o_qb = o_i + 2 * H_A
    o_ckv = o_qb + H_B * DH_B
    o_qi = o_ckv + R_KV
    o_ki = o_qi + H_IDX * D_IDX
    o_wi = o_ki + D_IDX
    gates = _pad_cols(jnp.concatenate([w[:, o_i:o_qb], w[:, o_wi:o_wi + H_IDX]], axis=1), LANES)
    wqi = w[:, o_qi:o_ki].reshape(d, H_IDX, D_IDX)
    wqi = jnp.concatenate([wqi, wqi], axis=-1).reshape(d, H_IDX * LANES)
    wki = w[:, o_ki:o_wi]
    return jnp.concatenate([w[:, :o_i], gates, w[:, o_qb:o_ckv], w[:, o_ckv:o_qi], wqi, wki, wki],
                           axis=1).astype(BF16)


def pack_cd_w_in(w):
    o_g = 2 * H_C * DK_C + H_C * DV_C
    o_r = o_g + GLA_RANK
    o_q = o_r + H_C * DV_C
    return jnp.concatenate([w[:, :o_g], w[:, o_r:o_q], _pad_cols(w[:, o_g:o_r], LANES), w[:, o_q:]],
                           axis=1).astype(BF16)


def kernel(x, c, w_ada, b_ada, g_mix, g_ffn, g_final, rel_bias, ab_w_in, ab_conv_w, ab_conv_b, ab_gate_b, ab_hnorm_g, ab_w_uk, ab_w_uv, ab_w_out, cd_w_in, cd_w_alpha, cd_b_alpha, cd_hnorm_g, cd_w_out, moe_w_coarse, moe_b_coarse, moe_w_fine, moe_b_fine, moe_w_gate, moe_w_up, moe_w_down):
    depth = w_ada.shape[0]
    L, d = x.shape[1], x.shape[2]
    mod = ada_mod(c, w_ada, b_ada)
    table = bias_table(rel_bias, L)
    for l in range(depth):
        sh_m, sc_m, gt_m, sh_f, sc_f, gt_f = [mod[l, :, i * d:(i + 1) * d] for i in range(6)]
        j = l // 2
        if l % 2 == 0:
            qk, va, oa, gates, qb, ckv, qi, ki = norm_proj(
                x, g_mix[l], sc_m, sh_m, pack_ab_w_in(ab_w_in[j]), _segments(AB_WIDTHS), AB_DTYPES)
            y1 = mlstm_mixer(qk, va, oa, gates, ab_conv_w[j], ab_conv_b[j], ab_gate_b[j], ab_hnorm_g[j],
                             out_dtype=BF16)
            y2 = dsa_mixer(qb, qi, gates, ki, ckv, ab_w_uk[j], ab_w_uv[j], table, out_dtype=BF16)
            w_out = ab_w_out[j]
        else:
            qk, vc, rc, gc, qd, kd, vd = norm_proj(
                x, g_mix[l], sc_m, sh_m, pack_cd_w_in(cd_w_in[j]), _segments(CD_WIDTHS), CD_DTYPES)
            y1 = gla_mixer(qk, vc, gc, rc, cd_w_alpha[j], cd_b_alpha[j], cd_hnorm_g[j], out_dtype=BF16)
            y2 = dilated_mixer(qd, kd, vd, rel_bias, out_dtype=BF16)
            w_out = cd_w_out[j]
        x = outproj_residual(x, y1, y2, gt_m, w_out.astype(BF16))
        x = moe_layer(x, g_ffn[l], sc_f, sh_f, gt_f, moe_w_coarse[l], moe_b_coarse[l], moe_w_fine[l],
                      moe_b_fine[l], moe_w_gate[l], moe_w_up[l], moe_w_down[l],
                      g_final=g_final if l == depth - 1 else None)
    return x
```
